```python
import math
import jax, jax.numpy as jnp
from jax import lax
import numpy as np

D_MODEL = 2048
BATCH = 2
SEQ = 16384
DEPTH = 1
DEC_BATCH = 1
DEC_SEQ = 16384
PAST_LEN = 128

D_HY = 1024
N_HEADS = 16
HEAD_DIM = 64
D_ATT = N_HEADS * HEAD_DIM
D_MIX = D_HY + D_ATT
IN_COLS = 3 * D_HY + 3 * D_ATT
SHORT_CONV = 3
HY_BANDS = 16
HY_POS_DIM = 1 + 2 * HY_BANDS
HY_FILT_HIDDEN = 64
HY_TARGET = 1e-2
HY_FAST_DECAY = 0.3
HY_SLOW_DECAY = 1.5
GRID_W = 64
NA_ROWS = 8
NA_COLS = 16
N_EXPERTS = 16
EXPERT_FF = 5504
CAPACITY_FACTOR = 2
N_ADA = 6
EPS = 1e-6
NEG_INF = -1e30

kernel_name = "hymba_hyena_natten_ec_encoder"


def rms_norm(x, g):
    xf = x.astype(jnp.float32)
    y = xf * lax.rsqrt(jnp.mean(xf * xf, axis=-1, keepdims=True) + EPS)
    return (y * g.astype(jnp.float32)).astype(x.dtype)


def hyena_filter_spectrum(L, w0, b0, w1, b1, w2, b2, w3, freq):
    f32 = jnp.float32
    pos = jnp.arange(L, dtype=f32)
    t = jnp.linspace(0.0, 1.0, L, dtype=f32)[:, None]
    bands = jnp.linspace(1e-4, HY_BANDS - 1, HY_BANDS, dtype=f32)
    ang = (2.0 * math.pi / L) * pos[:, None] * bands[None, :]
    z = jnp.concatenate([t, jnp.cos(ang), -jnp.sin(ang)], axis=-1)
    fr = freq.astype(f32)
    h = jnp.sin(fr * (z @ w0.astype(f32) + b0.astype(f32)))
    h = jnp.sin(fr * (h @ w1.astype(f32) + b1.astype(f32)))
    h = jnp.sin(fr * (h @ w2.astype(f32) + b2.astype(f32)))
    h = h @ w3.astype(f32)
    max_decay = math.log(HY_TARGET) / HY_FAST_DECAY
    min_decay = math.log(HY_TARGET) / HY_SLOW_DECAY
    deltas = jnp.abs(jnp.linspace(min_decay, max_decay, D_HY, dtype=f32))
    decay = jnp.exp(-t * deltas[None, :])
    h_fwd = h[:, :D_HY] * decay
    h_bwd = h[:, D_HY:] * decay
    taps = jnp.concatenate([h_fwd, jnp.zeros((1, D_HY), f32), h_bwd[:0:-1]], axis=0)
    taps = taps / jnp.sum(jnp.abs(taps), axis=0, keepdims=True)
    return jnp.fft.rfft(taps, axis=0)


def hyena_mixer(u, conv_w, conv_b, spec, skip):
    L = u.shape[1]
    up = jnp.pad(u, ((0, 0), (1, 1), (0, 0)))
    uc = conv_w[0] * up[:, :-2] + conv_w[1] * up[:, 1:-1] + conv_w[2] * up[:, 2:] + conv_b
    x1, x2, v = jnp.split(uc, 3, axis=-1)
    v = v * x2
    vf = jnp.fft.rfft(v.astype(jnp.float32), n=2 * L, axis=1)
    y = jnp.fft.irfft(vf * spec[None], n=2 * L, axis=1)[:, :L]
    v = y.astype(u.dtype) + skip * v
    return v * x1


def neighbourhood_attention(q, k, v, rpb):
    B, L, H, dh = q.shape
    rows = L // GRID_W
    kh = min(NA_ROWS, rows)
    qg = q.reshape(B, rows, GRID_W, H, dh)
    kg = k.reshape(B, rows, GRID_W, H, dh)
    vg = v.reshape(B, rows, GRID_W, H, dh)
    cols = jnp.arange(GRID_W)
    c0 = jnp.clip(cols - NA_COLS // 2, 0, GRID_W - NA_COLS)
    col_mask = (cols[None, :] >= c0[:, None]) & (cols[None, :] < c0[:, None] + NA_COLS)
    dc = jnp.clip(cols[None, :] - cols[:, None], -(NA_COLS - 1), NA_COLS - 1) + NA_COLS - 1
    rpb_cols = rpb.astype(jnp.float32)[:, :, dc]
    scale = dh ** -0.5
    offs = jnp.arange(kh)

    def row_block(r):
        r0 = jnp.clip(r - kh // 2, 0, rows - kh)
        q_r = lax.dynamic_index_in_dim(qg, r, axis=1, keepdims=False)
        k_b = lax.dynamic_slice_in_dim(kg, r0, kh, axis=1)
        v_b = lax.dynamic_slice_in_dim(vg, r0, kh, axis=1)
        s = jnp.einsum('bqhd,biwhd->bhqiw', q_r, k_b).astype(jnp.float32) * scale
        dr = r0 - r + offs + NA_ROWS - 1
        bias = jnp.take(rpb_cols, dr, axis=1)
        s = s + jnp.transpose(bias, (0, 2, 1, 3))[None]
        s = jnp.where(col_mask[None, None, :, None, :], s, NEG_INF)
        p = jax.nn.softmax(s.reshape(B, H, GRID_W, kh * GRID_W), axis=-1)
        p = p.reshape(B, H, GRID_W, kh, GRID_W).astype(v.dtype)
        return jnp.einsum('bhqiw,biwhd->bqhd', p, v_b)

    o = lax.map(row_block, jnp.arange(rows))
    return jnp.transpose(o, (1, 0, 2, 3, 4)).reshape(B, L, H * dh)


def expert_choice_ffn(h, w_router, w_gate, w_up, w_down):
    B, L, D = h.shape
    n_tok = B * L
    tok = h.reshape(n_tok, D)
    aff = jax.nn.softmax((tok @ w_router).astype(jnp.float32), axis=-1)
    cap = CAPACITY_FACTOR * n_tok // N_EXPERTS
    gates, idx = lax.top_k(aff.T, cap)

    def expert(args):
        wg, wu, wd, ix, gt = args
        xe = jnp.take(tok, ix, axis=0)
        he = jax.nn.silu(xe @ wg) * (xe @ wu)
        return (he @ wd) * gt[:, None].astype(tok.dtype)

    ye = lax.map(expert, (w_gate, w_up, w_down, idx, gates))
    out = jnp.zeros_like(tok).at[idx.reshape(-1)].add(ye.reshape(-1, D))
    return out.reshape(B, L, D)


def encoder_layer(x, c, ada_w, ada_b, norm1_g, w_in, hy_conv_w, hy_conv_b,
                  hy_f_w0, hy_f_b0, hy_f_w1, hy_f_b1, hy_f_w2, hy_f_b2, hy_f_w3, hy_f_freq, hy_skip,
                  q_norm_g, k_norm_g, rpb, out_norm_hy, out_norm_att, w_out, norm2_g,
                  w_router, w_gate, w_up, w_down):
    B, L, _ = x.shape
    ada = jax.nn.silu(c) @ ada_w + ada_b
    sh1, sc1, g1, sh2, sc2, g2 = jnp.split(ada[:, None, :], N_ADA, axis=-1)
    h = rms_norm(x, norm1_g) * (1 + sc1) + sh1
    proj = h @ w_in
    u_hy = proj[..., :3 * D_HY]
    q, k, v = jnp.split(proj[..., 3 * D_HY:], 3, axis=-1)
    spec = hyena_filter_spectrum(L, hy_f_w0, hy_f_b0, hy_f_w1, hy_f_b1, hy_f_w2, hy_f_b2, hy_f_w3, hy_f_freq)
    y_hy = hyena_mixer(u_hy, hy_conv_w, hy_conv_b, spec, hy_skip)
    q = rms_norm(q.reshape(B, L, N_HEADS, HEAD_DIM), q_norm_g)
    k = rms_norm(k.reshape(B, L, N_HEADS, HEAD_DIM), k_norm_g)
    v = v.reshape(B, L, N_HEADS, HEAD_DIM)
    y_at = neighbourhood_attention(q, k, v, rpb)
    mixed = jnp.concatenate([rms_norm(y_hy, out_norm_hy), rms_norm(y_at, out_norm_att)], axis=-1) @ w_out
    x = x + g1 * mixed
    h2 = rms_norm(x, norm2_g) * (1 + sc2) + sh2
    x = x + g2 * expert_choice_ffn(h2, w_router, w_gate, w_up, w_down)
    return x


def setup_inputs(seed: int = 0) -> dict:
    key = jax.random.key(seed)
    ks = jax.random.split(key, 32)
    f32 = jnp.float32

    def nrm(k, shape, scale):
        return jax.random.normal(k, shape, f32) * scale

    def gain(k, shape):
        return 1.0 + 0.05 * jax.random.normal(k, shape, f32)

    D = D_MODEL
    return {
        "x_prompt": nrm(ks[0], (BATCH, SEQ, D), 1.0),
        "x_sample": nrm(ks[1], (DEC_BATCH, DEC_SEQ, D), 1.0),
        "c_prompt": nrm(ks[2], (BATCH, D), 1.0),
        "c_sample": nrm(ks[3], (DEC_BATCH, D), 1.0),
        "ada_w": nrm(ks[4], (DEPTH, D, N_ADA * D), 0.5 * D ** -0.5),
        "ada_b": nrm(ks[5], (DEPTH, N_ADA * D), 0.02),
        "norm1_g": gain(ks[6], (DEPTH, D)),
        "w_in": nrm(ks[7], (DEPTH, D, IN_COLS), D ** -0.5),
        "hy_conv_w": nrm(ks[8], (DEPTH, SHORT_CONV, 3 * D_HY), SHORT_CONV ** -0.5),
        "hy_conv_b": nrm(ks[9], (DEPTH, 3 * D_HY), 0.02),
        "hy_f_w0": nrm(ks[10], (DEPTH, HY_POS_DIM, HY_FILT_HIDDEN), HY_POS_DIM ** -0.5),
        "hy_f_b0": nrm(ks[11], (DEPTH, HY_FILT_HIDDEN), 0.1),
        "hy_f_w1": nrm(ks[12], (DEPTH, HY_FILT_HIDDEN, HY_FILT_HIDDEN), HY_FILT_HIDDEN ** -0.5),
        "hy_f_b1": nrm(ks[13], (DEPTH, HY_FILT_HIDDEN), 0.1),
        "hy_f_w2": nrm(ks[14], (DEPTH, HY_FILT_HIDDEN, HY_FILT_HIDDEN), HY_FILT_HIDDEN ** -0.5),
        "hy_f_b2": nrm(ks[15], (DEPTH, HY_FILT_HIDDEN), 0.1),
        "hy_f_w3": nrm(ks[16], (DEPTH, HY_FILT_HIDDEN, 2 * D_HY), HY_FILT_HIDDEN ** -0.5),
        "hy_f_freq": gain(ks[17], (DEPTH, HY_FILT_HIDDEN)),
        "hy_skip": nrm(ks[18], (DEPTH, D_HY), 0.5),
        "q_norm_g": gain(ks[19], (DEPTH, HEAD_DIM)),
        "k_norm_g": gain(ks[20], (DEPTH, HEAD_DIM)),
        "rpb": nrm(ks[21], (DEPTH, N_HEADS, 2 * NA_ROWS - 1, 2 * NA_COLS - 1), 0.1),
        "out_norm_hy": gain(ks[22], (DEPTH, D_HY)),
        "out_norm_att": gain(ks[23], (DEPTH, D_ATT)),
        "w_out": nrm(ks[24], (DEPTH, D_MIX, D), D_MIX ** -0.5),
        "norm2_g": gain(ks[25], (DEPTH, D)),
        "w_router": nrm(ks[26], (DEPTH, D, N_EXPERTS), D ** -0.5),
        "w_gate": nrm(ks[27], (DEPTH, N_EXPERTS, D, EXPERT_FF), D ** -0.5),
        "w_up": nrm(ks[28], (DEPTH, N_EXPERTS, D, EXPERT_FF), D ** -0.5),
        "w_down": nrm(ks[29], (DEPTH, N_EXPERTS, EXPERT_FF, D), EXPERT_FF ** -0.5),
    }


def reference(x_prompt, x_sample, c_prompt, c_sample, ada_w, ada_b, norm1_g, w_in, hy_conv_w, hy_conv_b,
              hy_f_w0, hy_f_b0, hy_f_w1, hy_f_b1, hy_f_w2, hy_f_b2, hy_f_w3, hy_f_freq, hy_skip,
              q_norm_g, k_norm_g, rpb, out_norm_hy, out_norm_att, w_out, norm2_g,
              w_router, w_gate, w_up, w_down):
    y_prompt = x_prompt
    y_sample = x_sample
    for l in range(DEPTH):
        lp = (ada_w[l], ada_b[l], norm1_g[l], w_in[l], hy_conv_w[l], hy_conv_b[l],
              hy_f_w0[l], hy_f_b0[l], hy_f_w1[l], hy_f_b1[l], hy_f_w2[l], hy_f_b2[l], hy_f_w3[l],
              hy_f_freq[l], hy_skip[l], q_norm_g[l], k_norm_g[l], rpb[l], out_norm_hy[l], out_norm_att[l],
              w_out[l], norm2_g[l], w_router[l], w_gate[l], w_up[l], w_down[l])
        y_prompt = encoder_layer(y_prompt, c_prompt, *lp)
        y_sample = encoder_layer(y_sample, c_sample, *lp)
    return (y_prompt, y_sample)
```

```python
import functools
import math

import jax
import jax.numpy as jnp
from jax import lax
from jax.experimental import pallas as pl
from jax.experimental.pallas import tpu as pltpu

F32 = jnp.float32
BF16 = jnp.bfloat16

HEAD_DIM = 64
GRID_W = 64
NA_ROWS = 8
NA_COLS = 16
HY_BANDS = 16
HY_TARGET = 1e-2
HY_FAST_DECAY = 0.3
HY_SLOW_DECAY = 1.5
CAPACITY_FACTOR = 2
N_ADA = 6
EPS = 1e-6
NEG_INF = -1e30

LANES = 128
VMEM_LIMIT_BYTES = 56 * 1024 * 1024

ROWS_PER_STEP = 8
TOK_PER_STEP = ROWS_PER_STEP * GRID_W


def _largest_tile(n, pref, align):
    if n <= pref:
        return n
    t = (pref // align) * align
    while t > align and n % t:
        t -= align
    assert n % t == 0, (n, pref, align)
    return t


def _cparams(sem):
    return pltpu.CompilerParams(dimension_semantics=sem, vmem_limit_bytes=VMEM_LIMIT_BYTES)


def _ada_kernel(ct_ref, w_ref, b_ref, o_ref):
    ct = ct_ref[...]
    st = ct * jax.nn.sigmoid(ct)
    w = w_ref[...]
    for b in range(ct.shape[1]):
        o_ref[b:b + 1, :] = jnp.sum(w * st[:, b:b + 1], axis=0, keepdims=True) + b_ref[...]


def _ada(c, ada_w, ada_b):
    nb, d = c.shape
    n = ada_w.shape[1]
    tn = _largest_tile(n, 1024, LANES)
    return pl.pallas_call(
        _ada_kernel,
        grid=(n // tn,),
        in_specs=[
            pl.BlockSpec((d, nb), lambda j: (0, 0)),
            pl.BlockSpec((d, tn), lambda j: (0, j)),
            pl.BlockSpec((1, tn), lambda j: (0, j)),
        ],
        out_specs=pl.BlockSpec((nb, tn), lambda j: (0, j)),
        out_shape=jax.ShapeDtypeStruct((nb, n), F32),
        compiler_params=_cparams(("arbitrary",)),
        name="ada",
    )(c.T, ada_w, ada_b.reshape(1, n))


def _inproj_kernel(x_ref, g_ref, sc_ref, sh_ref, w_ref, qg_ref, kg_ref, bd_ref, o_ref, h_scr,
                   *, tn, q_col0, k_col0, v_col0):
    j = pl.program_id(2)

    @pl.when(j == 0)
    def _():
        x = x_ref[0]
        ms = jnp.mean(x * x, axis=-1, keepdims=True)
        y = (x * lax.rsqrt(ms + EPS)) * g_ref[...]
        h_scr[...] = (y * (1.0 + sc_ref[0]) + sh_ref[0]).astype(BF16)

    acc = jnp.dot(h_scr[...], w_ref[...], preferred_element_type=F32)
    col0 = j * tn
    is_q = jnp.logical_and(col0 >= q_col0, col0 < k_col0)
    is_k = jnp.logical_and(col0 >= k_col0, col0 < v_col0)
    is_qk = jnp.logical_or(is_q, is_k)

    @pl.when(jnp.logical_not(is_qk))
    def _():
        o_ref[0] = acc.astype(o_ref.dtype)

    @pl.when(is_qk)
    def _():
        gain = jnp.where(is_q, qg_ref[...], kg_ref[...])
        bd = bd_ref[...]
        for c in range(tn // LANES):
            sl = slice(c * LANES, (c + 1) * LANES)
            a = acc[:, sl]
            sq = a * a
            hi = sq.astype(BF16)
            lo = (sq - hi.astype(F32)).astype(BF16)
            ss = (jnp.dot(hi, bd, preferred_element_type=F32)
                  + jnp.dot(lo, bd, preferred_element_type=F32))
            y = (a * lax.rsqrt(ss * (1.0 / HEAD_DIM) + EPS)) * gain[:, sl]
            o_ref[0, :, sl] = y.astype(o_ref.dtype)


def _inproj(x, norm_g, sc, sh, w_bf16, q_gain_row, k_gain_row, d_hy, d_att):
    bsz, seq, d = x.shape
    n = w_bf16.shape[1]
    tm = _largest_tile(seq, 1024, 16)
    tn = _largest_tile(d_att, 1024, LANES)
    assert (3 * d_hy) % tn == 0 and n % tn == 0
    lane = jnp.arange(LANES)
    bd = (lane[:, None] // HEAD_DIM == lane[None, :] // HEAD_DIM).astype(BF16)
    kern = functools.partial(_inproj_kernel, tn=tn, q_col0=3 * d_hy, k_col0=3 * d_hy + d_att,
                             v_col0=3 * d_hy + 2 * d_att)
    return pl.pallas_call(
        kern,
        grid=(bsz, seq // tm, n // tn),
        in_specs=[
            pl.BlockSpec((1, tm, d), lambda b, i, j: (b, i, 0)),
            pl.BlockSpec((1, d), lambda b, i, j: (0, 0)),
            pl.BlockSpec((1, 1, d), lambda b, i, j: (b, 0, 0)),
            pl.BlockSpec((1, 1, d), lambda b, i, j: (b, 0, 0)),
            pl.BlockSpec((d, tn), lambda b, i, j: (0, j)),
            pl.BlockSpec((1, tn), lambda b, i, j: (0, 0)),
            pl.BlockSpec((1, tn), lambda b, i, j: (0, 0)),
            pl.BlockSpec((LANES, LANES), lambda b, i, j: (0, 0)),
        ],
        out_specs=pl.BlockSpec((1, tm, tn), lambda b, i, j: (b, i, j)),
        out_shape=jax.ShapeDtypeStruct((bsz, seq, n), BF16),
        scratch_shapes=[pltpu.VMEM((tm, d), BF16)],
        compiler_params=_cparams(("parallel", "parallel", "arbitrary")),
        name="inproj",
    )(x, norm_g.reshape(1, d), sc[:, None, :], sh[:, None, :], w_bf16,
      jnp.tile(q_gain_row, tn // HEAD_DIM).reshape(1, tn),
      jnp.tile(k_gain_row, tn // HEAD_DIM).reshape(1, tn), bd)


def _natten_kernel(q_ref, k0_ref, k1_ref, k2_ref, v0_ref, v1_ref, v2_ref, bias_ref, o_ref,
                   k_buf, v_buf, *, rows):
    i = pl.program_id(2)
    t = TOK_PER_STEP
    k_buf[0:t] = k0_ref[0]
    k_buf[t:2 * t] = k1_ref[0]
    k_buf[2 * t:3 * t] = k2_ref[0]
    v_buf[0:t] = v0_ref[0]
    v_buf[t:2 * t] = v1_ref[0]
    v_buf[2 * t:3 * t] = v2_ref[0]
    first_head = lax.broadcasted_iota(jnp.int32, (1, LANES), 1) < HEAD_DIM
    zero = jnp.zeros((), BF16)
    window = NA_ROWS * GRID_W
    for rr in range(ROWS_PER_STEP):
        r = i * ROWS_PER_STEP + rr
        r0 = jnp.clip(r - NA_ROWS // 2, 0, rows - NA_ROWS)
        off = pl.multiple_of((r0 - (i - 1) * ROWS_PER_STEP) * GRID_W, GRID_W)
        start = r0 - r + NA_ROWS - 1
        kw = k_buf[pl.ds(off, window), :]
        vw = v_buf[pl.ds(off, window), :]
        q = q_ref[0, rr * GRID_W:(rr + 1) * GRID_W, :]
        q2 = jnp.concatenate([jnp.where(first_head, q, zero), jnp.where(first_head, zero, q)], axis=0)
        s = lax.dot_general(q2, kw, (((1,), (1,)), ((), ())), preferred_element_type=F32)
        s = s + jnp.concatenate([bias_ref[0, 0, start], bias_ref[0, 1, start]], axis=0)
        m = jnp.max(s, axis=-1, keepdims=True)
        p = jnp.exp(s - m)
        l = jnp.sum(p, axis=-1, keepdims=True)
        o2 = jnp.dot(p.astype(BF16), vw, preferred_element_type=F32) / l
        o = jnp.where(first_head, o2[:GRID_W], o2[GRID_W:])
        o_ref[0, rr * GRID_W:(rr + 1) * GRID_W, :] = o.astype(o_ref.dtype)


def _natten_bias(rpb):
    h = rpb.shape[0]
    cols = jnp.arange(GRID_W)
    c0 = jnp.clip(cols - NA_COLS // 2, 0, GRID_W - NA_COLS)
    col_mask = (cols[None, :] >= c0[:, None]) & (cols[None, :] < c0[:, None] + NA_COLS)
    dc = jnp.clip(cols[None, :] - cols[:, None], -(NA_COLS - 1), NA_COLS - 1) + NA_COLS - 1
    rpb_cols = rpb.astype(F32)[:, :, dc]
    dr = jnp.arange(NA_ROWS)[:, None] + jnp.arange(NA_ROWS)[None, :]
    tab = rpb_cols[:, dr]
    tab = jnp.where(col_mask[None, None, None], tab, NEG_INF)
    tab = jnp.transpose(tab, (0, 1, 3, 2, 4)).reshape(h // 2, 2, NA_ROWS, GRID_W, NA_ROWS * GRID_W)
    return tab


def _natten(proj, bias_tab, d_hy, d_att):
    bsz, seq, _ = proj.shape
    rows = seq // GRID_W
    nblk = rows // ROWS_PER_STEP
    assert rows % ROWS_PER_STEP == 0 and rows >= NA_ROWS and nblk >= 2
    hp = d_att // LANES
    qc, kc, vc = (3 * d_hy) // LANES, (3 * d_hy + d_att) // LANES, (3 * d_hy + 2 * d_att) // LANES
    t = TOK_PER_STEP

    def blk(col, shift):
        return pl.BlockSpec(
            (1, t, LANES),
            lambda b, p, i: (b, jnp.clip(i + shift, 0, nblk - 1), col + p))

    return pl.pallas_call(
        functools.partial(_natten_kernel, rows=rows),
        grid=(bsz, hp, nblk),
        in_specs=[blk(qc, 0), blk(kc, -1), blk(kc, 0), blk(kc, 1), blk(vc, -1), blk(vc, 0), blk(vc, 1),
                  pl.BlockSpec((1, 2, NA_ROWS, GRID_W, NA_ROWS * GRID_W), lambda b, p, i: (p, 0, 0, 0, 0))],
        out_specs=pl.BlockSpec((1, t, LANES), lambda b, p, i: (b, i, p)),
        out_shape=jax.ShapeDtypeStruct((bsz, seq, d_att), BF16),
        scratch_shapes=[pltpu.VMEM((3 * t, LANES), BF16), pltpu.VMEM((3 * t, LANES), BF16)],
        compiler_params=_cparams(("parallel", "parallel", "arbitrary")),
        name="natten",
    )(proj, proj, proj, proj, proj, proj, proj, bias_tab)


def _outproj_kernel(yh_ref, ya_ref, x_ref, gh_ref, ga_ref, wh_ref, wa_ref, g1_ref, n2_ref, sc_ref, sh_ref,
                    wr_ref, x1_ref, h2_ref, aff_ref):
    def _norm(y_ref, g_ref):
        y = y_ref[0].astype(F32)
        ms = jnp.mean(y * y, axis=-1, keepdims=True)
        return ((y * lax.rsqrt(ms + EPS)) * g_ref[...]).astype(BF16)

    mixed = (jnp.dot(_norm(yh_ref, gh_ref), wh_ref[...], preferred_element_type=F32)
             + jnp.dot(_norm(ya_ref, ga_ref), wa_ref[...], preferred_element_type=F32))
    x1 = x_ref[0] + g1_ref[0] * mixed
    x1_ref[0] = x1
    ms = jnp.mean(x1 * x1, axis=-1, keepdims=True)
    h2 = ((x1 * lax.rsqrt(ms + EPS)) * n2_ref[...]) * (1.0 + sc_ref[0]) + sh_ref[0]
    h2_ref[0] = h2.astype(BF16)
    wr = wr_ref[...]
    w_hi = wr.astype(BF16)
    w_lo = (wr - w_hi.astype(F32)).astype(BF16)
    h_hi = h2.astype(BF16)
    h_lo = (h2 - h_hi.astype(F32)).astype(BF16)
    logits = (jnp.dot(h_hi, w_hi, preferred_element_type=F32)
              + jnp.dot(h_lo, w_hi, preferred_element_type=F32)
              + jnp.dot(h_hi, w_lo, preferred_element_type=F32))
    m = jnp.max(logits, axis=-1, keepdims=True)
    e = jnp.exp(logits - m)
    aff_ref[0] = e / jnp.sum(e, axis=-1, keepdims=True)


def _outproj(y_hy, y_at, x, gain_hy, gain_att, w_out_bf16, g1, norm2_g, sc2, sh2, w_router):
    bsz, seq, d = x.shape
    d_hy, d_att = y_hy.shape[-1], y_at.shape[-1]
    n_exp = w_router.shape[1]
    tm = _largest_tile(seq, 512, 16)
    row = lambda v: v[:, None, :]
    per_b = pl.BlockSpec((1, 1, d), lambda b, i: (b, 0, 0))
    const = lambda shape: pl.BlockSpec(shape, lambda b, i: tuple(0 for _ in shape))
    return pl.pallas_call(
        _outproj_kernel,
        grid=(bsz, seq // tm),
        in_specs=[
            pl.BlockSpec((1, tm, d_hy), lambda b, i: (b, i, 0)),
            pl.BlockSpec((1, tm, d_att), lambda b, i: (b, i, 0)),
            pl.BlockSpec((1, tm, d), lambda b, i: (b, i, 0)),
            const((1, d_hy)), const((1, d_att)),
            const((d_hy, d)), const((d_att, d)),
            per_b, const((1, d)), per_b, per_b,
            const((d, n_exp)),
        ],
        out_specs=[
            pl.BlockSpec((1, tm, d), lambda b, i: (b, i, 0)),
            pl.BlockSpec((1, tm, d), lambda b, i: (b, i, 0)),
            pl.BlockSpec((1, tm, n_exp), lambda b, i: (b, i, 0)),
        ],
        out_shape=[
            jax.ShapeDtypeStruct((bsz, seq, d), F32),
            jax.ShapeDtypeStruct((bsz, seq, d), BF16),
            jax.ShapeDtypeStruct((bsz, seq, n_exp), F32),
        ],
        compiler_params=_cparams(("parallel", "arbitrary")),
        name="outproj",
    )(y_hy, y_at, x, gain_hy.reshape(1, d_hy), gain_att.reshape(1, d_att),
      w_out_bf16[:d_hy], w_out_bf16[d_hy:], row(g1), norm2_g.reshape(1, d), row(sc2), row(sh2), w_router)


FF_TILE = 512


def _ffn_kernel(x_ref, wgu_ref, wd_ref, gt_ref, o_ref, acc_ref):
    f = pl.program_id(2)
    gu = jnp.dot(x_ref[0], wgu_ref[0], preferred_element_type=F32)
    g = gu[:, :FF_TILE]
    u = gu[:, FF_TILE:]
    he = ((g * jax.nn.sigmoid(g)) * u).astype(BF16)
    part = jnp.dot(he, wd_ref[0], preferred_element_type=F32)

    @pl.when(f == 0)
    def _():
        acc_ref[...] = part

    @pl.when(f > 0)
    def _():
        acc_ref[...] += part

    @pl.when(f == pl.num_programs(2) - 1)
    def _():
        o_ref[0] = (acc_ref[...] * gt_ref[0]).astype(o_ref.dtype)


def _prep_expert_weights(w_gate, w_up, w_down):
    n_exp, d, ff = w_gate.shape
    nf = -(-ff // FF_TILE)
    pad = nf * FF_TILE - ff
    wg = jnp.pad(w_gate.astype(BF16), ((0, 0), (0, 0), (0, pad))).reshape(n_exp, d, nf, 1, FF_TILE)
    wu = jnp.pad(w_up.astype(BF16), ((0, 0), (0, 0), (0, pad))).reshape(n_exp, d, nf, 1, FF_TILE)
    wgu = jnp.concatenate([wg, wu], axis=3).reshape(n_exp, d, nf * 2 * FF_TILE)
    wd = jnp.pad(w_down.astype(BF16), ((0, 0), (0, pad), (0, 0)))
    return wgu, wd


def _expert_ffn(xg, wgu, wd, gates):
    n_exp, cap, d = xg.shape
    nf = wd.shape[1] // FF_TILE
    tm = _largest_tile(cap, 1024, 16)
    return pl.pallas_call(
        _ffn_kernel,
        grid=(n_exp, cap // tm, nf),
        in_specs=[
            pl.BlockSpec((1, tm, d), lambda e, m, f: (e, m, 0)),
            pl.BlockSpec((1, d, 2 * FF_TILE), lambda e, m, f: (e, 0, f)),
            pl.BlockSpec((1, FF_TILE, d), lambda e, m, f: (e, f, 0)),
            pl.BlockSpec((1, tm, 1), lambda e, m, f: (e, m, 0)),
        ],
        out_specs=pl.BlockSpec((1, tm, d), lambda e, m, f: (e, m, 0)),
        out_shape=jax.ShapeDtypeStruct((n_exp, cap, d), BF16),
        scratch_shapes=[pltpu.VMEM((tm, d), F32)],
        compiler_params=_cparams(("parallel", "parallel", "arbitrary")),
        name="expert_ffn",
    )(xg, wgu, wd, gates[:, :, None])


def _hyena_filter_spectrum(seq, w0, b0, w1, b1, w2, b2, w3, freq, d_hy):
    pos = jnp.arange(seq, dtype=F32)
    t = jnp.linspace(0.0, 1.0, seq, dtype=F32)[:, None]
    bands = jnp.linspace(1e-4, HY_BANDS - 1, HY_BANDS, dtype=F32)
    ang = (2.0 * math.pi / seq) * pos[:, None] * bands[None, :]
    z = jnp.concatenate([t, jnp.cos(ang), -jnp.sin(ang)], axis=-1)
    hi = lax.Precision.HIGHEST
    h = jnp.sin(freq * (jnp.dot(z, w0, precision=hi) + b0))
    h = jnp.sin(freq * (jnp.dot(h, w1, precision=hi) + b1))
    h = jnp.sin(freq * (jnp.dot(h, w2, precision=hi) + b2))
    h = jnp.dot(h, w3, precision=hi)
    max_decay = math.log(HY_TARGET) / HY_FAST_DECAY
    min_decay = math.log(HY_TARGET) / HY_SLOW_DECAY
    deltas = jnp.abs(jnp.linspace(min_decay, max_decay, d_hy, dtype=F32))
    decay = jnp.exp(-t * deltas[None, :])
    h_fwd = h[:, :d_hy] * decay
    h_bwd = h[:, d_hy:] * decay
    taps = jnp.concatenate([h_fwd, jnp.zeros((1, d_hy), F32), h_bwd[:0:-1]], axis=0)
    taps = taps / jnp.sum(jnp.abs(taps), axis=0, keepdims=True)
    return jnp.fft.rfft(taps, axis=0)


def _hyena_mixer(u, conv_w, conv_b, spec, skip):
    seq = u.shape[1]
    up = jnp.pad(u, ((0, 0), (1, 1), (0, 0)))
    uc = conv_w[0] * up[:, :-2] + conv_w[1] * up[:, 1:-1] + conv_w[2] * up[:, 2:] + conv_b
    x1, x2, v = jnp.split(uc, 3, axis=-1)
    v = v * x2
    vf = jnp.fft.rfft(v, n=2 * seq, axis=1)
    y = jnp.fft.irfft(vf * spec[None], n=2 * seq, axis=1)[:, :seq]
    return (y + skip * v) * x1


def _layer(x, ada, p):
    bsz, seq, d = x.shape
    d_hy = p["hy_skip"].shape[0]
    d_att = p["out_norm_att"].shape[0]
    n_exp = p["w_router"].shape[1]
    sh1, sc1, g1, sh2, sc2, g2 = jnp.split(ada, N_ADA, axis=-1)

    q_gain = p["q_norm_g"] * (HEAD_DIM ** -0.5)
    proj = _inproj(x, p["norm1_g"], sc1, sh1, p["w_in_bf16"], q_gain, p["k_norm_g"], d_hy, d_att)

    u_hy = proj[..., :3 * d_hy].astype(F32)
    y_hy = _hyena_mixer(u_hy, p["hy_conv_w"], p["hy_conv_b"], p["spec"], p["hy_skip"]).astype(BF16)
    y_at = _natten(proj, p["bias_tab"], d_hy, d_att)

    x1, h2, aff = _outproj(y_hy, y_at, x, p["out_norm_hy"], p["out_norm_att"], p["w_out_bf16"],
                           g1, p["norm2_g"], sc2, sh2, p["w_router"])

    n_tok = bsz * seq
    cap = CAPACITY_FACTOR * n_tok // n_exp
    gates, idx = lax.top_k(aff.reshape(n_tok, n_exp).T, cap)
    xg = jnp.take(h2.reshape(n_tok, d), idx, axis=0)
    ye = _expert_ffn(xg, p["wgu"], p["wd"], gates)
    routed = jnp.zeros((n_tok, d), F32).at[idx.reshape(-1)].add(ye.reshape(-1, d).astype(F32))
    return x1 + g2[:, None, :] * routed.reshape(bsz, seq, d)


def kernel(x_prompt, x_sample, c_prompt, c_sample, ada_w, ada_b, norm1_g, w_in, hy_conv_w, hy_conv_b, hy_f_w0, hy_f_b0, hy_f_w1, hy_f_b1, hy_f_w2, hy_f_b2, hy_f_w3, hy_f_freq, hy_skip, q_norm_g, k_norm_g, rpb, out_norm_hy, out_norm_att, w_out, norm2_g, w_router, w_gate, w_up, w_down):
    depth = ada_w.shape[0]
    y_prompt, y_sample = x_prompt, x_sample
    nbp = x_prompt.shape[0]
    assert x_prompt.shape[1] == x_sample.shape[1]
    seq = x_prompt.shape[1]
    for l in range(depth):
        d_hy = hy_skip.shape[-1]
        wgu, wd = _prep_expert_weights(w_gate[l], w_up[l], w_down[l])
        p = {
            "norm1_g": norm1_g[l], "w_in_bf16": w_in[l].astype(BF16),
            "hy_conv_w": hy_conv_w[l], "hy_conv_b": hy_conv_b[l], "hy_skip": hy_skip[l],
            "spec": _hyena_filter_spectrum(seq, hy_f_w0[l], hy_f_b0[l], hy_f_w1[l], hy_f_b1[l], hy_f_w2[l],
                                           hy_f_b2[l], hy_f_w3[l], hy_f_freq[l], d_hy),
            "q_norm_g": q_norm_g[l], "k_norm_g": k_norm_g[l], "bias_tab": _natten_bias(rpb[l]),
            "out_norm_hy": out_norm_hy[l], "out_norm_att": out_norm_att[l],
            "w_out_bf16": w_out[l].astype(BF16), "norm2_g": norm2_g[l], "w_router": w_router[l],
            "wgu": wgu, "wd": wd,
        }
        ada = _ada(jnp.concatenate([c_prompt, c_sample], axis=0), ada_w[l], ada_b[l])
        y_prompt = _layer(y_prompt, ada[:nbp], p)
        y_sample = _layer(y_sample, ada[nbp:], p)
    return (y_prompt, y_sample)
```

```python
import functools
import math

import jax
import jax.numpy as jnp
from jax import lax
from jax.experimental import pallas as pl
from jax.experimental.pallas import tpu as pltpu

F32 = jnp.float32
BF16 = jnp.bfloat16

HEAD_DIM = 64
GRID_W = 64
NA_ROWS = 8
NA_COLS = 16
HY_BANDS = 16
HY_TARGET = 1e-2
HY_FAST_DECAY = 0.3
HY_SLOW_DECAY = 1.5
CAPACITY_FACTOR = 2
N_ADA = 6
EPS = 1e-6
NEG_INF = -1e30

LANES = 128
VMEM_LIMIT_BYTES = 56 * 1024 * 1024

ROWS_PER_STEP = 8
TOK_PER_STEP = ROWS_PER_STEP * GRID_W


def _largest_tile(n, pref, align):
    if n <= pref:
        return n
    t = (pref // align) * align
    while t > align and n % t:
        t -= align
    assert n % t == 0, (n, pref, align)
    return t


def _cparams(sem):
    return pltpu.CompilerParams(dimension_semantics=sem, vmem_limit_bytes=VMEM_LIMIT_BYTES)


def _ada_kernel(ct_ref, w_ref, b_ref, o_ref):
    ct = ct_ref[...]
    st = ct * jax.nn.sigmoid(ct)
    w = w_ref[...]
    for b in range(ct.shape[1]):
        o_ref[b:b + 1, :] = jnp.sum(w * st[:, b:b + 1], axis=0, keepdims=True) + b_ref[...]


def _ada(c, ada_w, ada_b):
    nb, d = c.shape
    n = ada_w.shape[1]
    tn = _largest_tile(n, 1024, LANES)
    return pl.pallas_call(
        _ada_kernel,
        grid=(n // tn,),
        in_specs=[
            pl.BlockSpec((d, nb), lambda j: (0, 0)),
            pl.BlockSpec((d, tn), lambda j: (0, j)),
            pl.BlockSpec((1, tn), lambda j: (0, j)),
        ],
        out_specs=pl.BlockSpec((nb, tn), lambda j: (0, j)),
        out_shape=jax.ShapeDtypeStruct((nb, n), F32),
        compiler_params=_cparams(("arbitrary",)),
        name="ada",
    )(c.T, ada_w, ada_b.reshape(1, n))


def _inproj_kernel(x_ref, g_ref, sc_ref, sh_ref, w_ref, qg_ref, kg_ref, bd_ref, o_ref, h_scr,
                   *, tn, q_col0, k_col0, v_col0):
    j = pl.program_id(2)

    @pl.when(j == 0)
    def _():
        x = x_ref[0]
        ms = jnp.mean(x * x, axis=-1, keepdims=True)
        y = (x * lax.rsqrt(ms + EPS)) * g_ref[...]
        h_scr[...] = (y * (1.0 + sc_ref[0]) + sh_ref[0]).astype(BF16)

    acc = jnp.dot(h_scr[...], w_ref[...], preferred_element_type=F32)
    col0 = j * tn
    is_q = jnp.logical_and(col0 >= q_col0, col0 < k_col0)
    is_k = jnp.logical_and(col0 >= k_col0, col0 < v_col0)
    is_qk = jnp.logical_or(is_q, is_k)

    @pl.when(jnp.logical_not(is_qk))
    def _():
        o_ref[0] = acc.astype(o_ref.dtype)

    @pl.when(is_qk)
    def _():
        gain = jnp.where(is_q, qg_ref[...], kg_ref[...])
        bd = bd_ref[...]
        for c in range(tn // LANES):
            sl = slice(c * LANES, (c + 1) * LANES)
            a = acc[:, sl]
            sq = a * a
            hi = sq.astype(BF16)
            lo = (sq - hi.astype(F32)).astype(BF16)
            ss = (jnp.dot(hi, bd, preferred_element_type=F32)
                  + jnp.dot(lo, bd, preferred_element_type=F32))
            y = (a * lax.rsqrt(ss * (1.0 / HEAD_DIM) + EPS)) * gain[:, sl]
            o_ref[0, :, sl] = y.astype(o_ref.dtype)


def _inproj(x, norm_g, sc, sh, w_bf16, q_gain_row, k_gain_row, d_hy, d_att):
    bsz, seq, d = x.shape
    n = w_bf16.shape[1]
    tm = _largest_tile(seq, 1024, 16)
    tn = _largest_tile(d_att, 1024, LANES)
    assert (3 * d_hy) % tn == 0 and n % tn == 0
    lane = jnp.arange(LANES)
    bd = (lane[:, None] // HEAD_DIM == lane[None, :] // HEAD_DIM).astype(BF16)
    kern = functools.partial(_inproj_kernel, tn=tn, q_col0=3 * d_hy, k_col0=3 * d_hy + d_att,
                             v_col0=3 * d_hy + 2 * d_att)
    return pl.pallas_call(
        kern,
        grid=(bsz, seq // tm, n // tn),
        in_specs=[
            pl.BlockSpec((1, tm, d), lambda b, i, j: (b, i, 0)),
            pl.BlockSpec((1, d), lambda b, i, j: (0, 0)),
            pl.BlockSpec((1, 1, d), lambda b, i, j: (b, 0, 0)),
            pl.BlockSpec((1, 1, d), lambda b, i, j: (b, 0, 0)),
            pl.BlockSpec((d, tn), lambda b, i, j: (0, j)),
            pl.BlockSpec((1, tn), lambda b, i, j: (0, 0)),
            pl.BlockSpec((1, tn), lambda b, i, j: (0, 0)),
            pl.BlockSpec((LANES, LANES), lambda b, i, j: (0, 0)),
        ],
        out_specs=pl.BlockSpec((1, tm, tn), lambda b, i, j: (b, i, j)),
        out_shape=jax.ShapeDtypeStruct((bsz, seq, n), BF16),
        scratch_shapes=[pltpu.VMEM((tm, d), BF16)],
        compiler_params=_cparams(("parallel", "parallel", "arbitrary")),
        name="inproj",
    )(x, norm_g.reshape(1, d), sc[:, None, :], sh[:, None, :], w_bf16,
      jnp.tile(q_gain_row, tn // HEAD_DIM).reshape(1, tn),
      jnp.tile(k_gain_row, tn // HEAD_DIM).reshape(1, tn), bd)


def _natten_kernel(q_ref, k0_ref, k1_ref, k2_ref, v0_ref, v1_ref, v2_ref, bias_ref, o_ref,
                   k_buf, v_buf, *, rows):
    i = pl.program_id(2)
    t = TOK_PER_STEP
    k_buf[0:t] = k0_ref[0]
    k_buf[t:2 * t] = k1_ref[0]
    k_buf[2 * t:3 * t] = k2_ref[0]
    v_buf[0:t] = v0_ref[0]
    v_buf[t:2 * t] = v1_ref[0]
    v_buf[2 * t:3 * t] = v2_ref[0]
    first_head = lax.broadcasted_iota(jnp.int32, (1, LANES), 1) < HEAD_DIM
    zero = jnp.zeros((), BF16)
    window = NA_ROWS * GRID_W
    for rr in range(ROWS_PER_STEP):
        r = i * ROWS_PER_STEP + rr
        r0 = jnp.clip(r - NA_ROWS // 2, 0, rows - NA_ROWS)
        off = pl.multiple_of((r0 - (i - 1) * ROWS_PER_STEP) * GRID_W, GRID_W)
        start = r0 - r + NA_ROWS - 1
        kw = k_buf[pl.ds(off, window), :]
        vw = v_buf[pl.ds(off, window), :]
        q = q_ref[0, rr * GRID_W:(rr + 1) * GRID_W, :]
        q2 = jnp.concatenate([jnp.where(first_head, q, zero), jnp.where(first_head, zero, q)], axis=0)
        s = lax.dot_general(q2, kw, (((1,), (1,)), ((), ())), preferred_element_type=F32)
        s = s + jnp.concatenate([bias_ref[0, 0, start], bias_ref[0, 1, start]], axis=0)
        m = jnp.max(s, axis=-1, keepdims=True)
        p = jnp.exp(s - m)
        l = jnp.sum(p, axis=-1, keepdims=True)
        o2 = jnp.dot(p.astype(BF16), vw, preferred_element_type=F32) / l
        o = jnp.where(first_head, o2[:GRID_W], o2[GRID_W:])
        o_ref[0, rr * GRID_W:(rr + 1) * GRID_W, :] = o.astype(o_ref.dtype)


def _natten_bias(rpb):
    h = rpb.shape[0]
    cols = jnp.arange(GRID_W)
    c0 = jnp.clip(cols - NA_COLS // 2, 0, GRID_W - NA_COLS)
    col_mask = (cols[None, :] >= c0[:, None]) & (cols[None, :] < c0[:, None] + NA_COLS)
    dc = jnp.clip(cols[None, :] - cols[:, None], -(NA_COLS - 1), NA_COLS - 1) + NA_COLS - 1
    rpb_cols = rpb.astype(F32)[:, :, dc]
    dr = jnp.arange(NA_ROWS)[:, None] + jnp.arange(NA_ROWS)[None, :]
    tab = rpb_cols[:, dr]
    tab = jnp.where(col_mask[None, None, None], tab, NEG_INF)
    tab = jnp.transpose(tab, (0, 1, 3, 2, 4)).reshape(h // 2, 2, NA_ROWS, GRID_W, NA_ROWS * GRID_W)
    return tab


def _natten(proj, bias_tab, d_hy, d_att):
    bsz, seq, _ = proj.shape
    rows = seq // GRID_W
    nblk = rows // ROWS_PER_STEP
    assert rows % ROWS_PER_STEP == 0 and rows >= NA_ROWS and nblk >= 2
    hp = d_att // LANES
    qc, kc, vc = (3 * d_hy) // LANES, (3 * d_hy + d_att) // LANES, (3 * d_hy + 2 * d_att) // LANES
    t = TOK_PER_STEP

    def blk(col, shift):
        return pl.BlockSpec(
            (1, t, LANES),
            lambda b, p, i: (b, jnp.clip(i + shift, 0, nblk - 1), col + p))

    return pl.pallas_call(
        functools.partial(_natten_kernel, rows=rows),
        grid=(bsz, hp, nblk),
        in_specs=[blk(qc, 0), blk(kc, -1), blk(kc, 0), blk(kc, 1), blk(vc, -1), blk(vc, 0), blk(vc, 1),
                  pl.BlockSpec((1, 2, NA_ROWS, GRID_W, NA_ROWS * GRID_W), lambda b, p, i: (p, 0, 0, 0, 0))],
        out_specs=pl.BlockSpec((1, t, LANES), lambda b, p, i: (b, i, p)),
        out_shape=jax.ShapeDtypeStruct((bsz, seq, d_att), BF16),
        scratch_shapes=[pltpu.VMEM((3 * t, LANES), BF16), pltpu.VMEM((3 * t, LANES), BF16)],
        compiler_params=_cparams(("parallel", "parallel", "arbitrary")),
        name="natten",
    )(proj, proj, proj, proj, proj, proj, proj, bias_tab)


def _outproj_kernel(yh_ref, ya_ref, x_ref, gh_ref, ga_ref, wh_ref, wa_ref, g1_ref, n2_ref, sc_ref, sh_ref,
                    wr_ref, x1_ref, h2_ref, aff_ref):
    def _norm(y_ref, g_ref):
        y = y_ref[0].astype(F32)
        ms = jnp.mean(y * y, axis=-1, keepdims=True)
        return ((y * lax.rsqrt(ms + EPS)) * g_ref[...]).astype(BF16)

    mixed = (jnp.dot(_norm(yh_ref, gh_ref), wh_ref[...], preferred_element_type=F32)
             + jnp.dot(_norm(ya_ref, ga_ref), wa_ref[...], preferred_element_type=F32))
    x1 = x_ref[0] + g1_ref[0] * mixed
    x1_ref[0] = x1
    ms = jnp.mean(x1 * x1, axis=-1, keepdims=True)
    h2 = ((x1 * lax.rsqrt(ms + EPS)) * n2_ref[...]) * (1.0 + sc_ref[0]) + sh_ref[0]
    h2_ref[0] = h2.astype(BF16)
    wr = wr_ref[...]
    w_hi = wr.astype(BF16)
    w_lo = (wr - w_hi.astype(F32)).astype(BF16)
    h_hi = h2.astype(BF16)
    h_lo = (h2 - h_hi.astype(F32)).astype(BF16)
    logits = (jnp.dot(h_hi, w_hi, preferred_element_type=F32)
              + jnp.dot(h_lo, w_hi, preferred_element_type=F32)
              + jnp.dot(h_hi, w_lo, preferred_element_type=F32))
    m = jnp.max(logits, axis=-1, keepdims=True)
    e = jnp.exp(logits - m)
    aff_ref[0] = e / jnp.sum(e, axis=-1, keepdims=True)


def _outproj(y_hy, y_at, x, gain_hy, gain_att, w_out_bf16, g1, norm2_g, sc2, sh2, w_router):
    bsz, seq, d = x.shape
    d_hy, d_att = y_hy.shape[-1], y_at.shape[-1]
    n_exp = w_router.shape[1]
    tm = _largest_tile(seq, 512, 16)
    row = lambda v: v[:, None, :]
    per_b = pl.BlockSpec((1, 1, d), lambda b, i: (b, 0, 0))
    const = lambda shape: pl.BlockSpec(shape, lambda b, i: tuple(0 for _ in shape))
    return pl.pallas_call(
        _outproj_kernel,
        grid=(bsz, seq // tm),
        in_specs=[
            pl.BlockSpec((1, tm, d_hy), lambda b, i: (b, i, 0)),
            pl.BlockSpec((1, tm, d_att), lambda b, i: (b, i, 0)),
            pl.BlockSpec((1, tm, d), lambda b, i: (b, i, 0)),
            const((1, d_hy)), const((1, d_att)),
            const((d_hy, d)), const((d_att, d)),
            per_b, const((1, d)), per_b, per_b,
            const((d, n_exp)),
        ],
        out_specs=[
            pl.BlockSpec((1, tm, d), lambda b, i: (b, i, 0)),
            pl.BlockSpec((1, tm, d), lambda b, i: (b, i, 0)),
            pl.BlockSpec((1, tm, n_exp), lambda b, i: (b, i, 0)),
        ],
        out_shape=[
            jax.ShapeDtypeStruct((bsz, seq, d), F32),
            jax.ShapeDtypeStruct((bsz, seq, d), BF16),
            jax.ShapeDtypeStruct((bsz, seq, n_exp), F32),
        ],
        compiler_params=_cparams(("parallel", "arbitrary")),
        name="outproj",
    )(y_hy, y_at, x, gain_hy.reshape(1, d_hy), gain_att.reshape(1, d_att),
      w_out_bf16[:d_hy], w_out_bf16[d_hy:], row(g1), norm2_g.reshape(1, d), row(sc2), row(sh2), w_router)


FF_TILE = 512


def _ffn_kernel(x_ref, wg_ref, wu_ref, wd_ref, gt_ref, o_ref, acc_ref, *, ff):
    f = pl.program_id(2)
    valid = ff - f * FF_TILE
    x = x_ref[0]
    g = jnp.dot(x, wg_ref[0], preferred_element_type=F32)
    u = jnp.dot(x, wu_ref[0], preferred_element_type=F32)
    col = lax.broadcasted_iota(jnp.int32, (1, FF_TILE), 1)
    he = jnp.where(col < valid, (g * jax.nn.sigmoid(g)) * u, 0.0).astype(BF16)
    row = lax.broadcasted_iota(jnp.int32, (FF_TILE, 1), 0)
    wd = jnp.where(row < valid, wd_ref[0], jnp.zeros((), BF16))
    part = jnp.dot(he, wd, preferred_element_type=F32)

    @pl.when(f == 0)
    def _():
        acc_ref[...] = part

    @pl.when(f > 0)
    def _():
        acc_ref[...] += part

    @pl.when(f == pl.num_programs(2) - 1)
    def _():
        o_ref[0] = (acc_ref[...] * gt_ref[0]).astype(o_ref.dtype)


def _expert_ffn(xg, wg, wu, wd, gates):
    n_exp, cap, d = xg.shape
    ff = wg.shape[2]
    tm = _largest_tile(cap, 1024, 16)
    return pl.pallas_call(
        functools.partial(_ffn_kernel, ff=ff),
        grid=(n_exp, cap // tm, pl.cdiv(ff, FF_TILE)),
        in_specs=[
            pl.BlockSpec((1, tm, d), lambda e, m, f: (e, m, 0)),
            pl.BlockSpec((1, d, FF_TILE), lambda e, m, f: (e, 0, f)),
            pl.BlockSpec((1, d, FF_TILE), lambda e, m, f: (e, 0, f)),
            pl.BlockSpec((1, FF_TILE, d), lambda e, m, f: (e, f, 0)),
            pl.BlockSpec((1, tm, 1), lambda e, m, f: (e, m, 0)),
        ],
        out_specs=pl.BlockSpec((1, tm, d), lambda e, m, f: (e, m, 0)),
        out_shape=jax.ShapeDtypeStruct((n_exp, cap, d), BF16),
        scratch_shapes=[pltpu.VMEM((tm, d), F32)],
        compiler_params=_cparams(("parallel", "parallel", "arbitrary")),
        name="expert_ffn",
    )(xg, wg, wu, wd, gates[:, :, None])


def _hyena_filter_spectrum(seq, w0, b0, w1, b1, w2, b2, w3, freq, d_hy):
    pos = jnp.arange(seq, dtype=F32)
    t = jnp.linspace(0.0, 1.0, seq, dtype=F32)[:, None]
    bands = jnp.linspace(1e-4, HY_BANDS - 1, HY_BANDS, dtype=F32)
    ang = (2.0 * math.pi / seq) * pos[:, None] * bands[None, :]
    z = jnp.concatenate([t, jnp.cos(ang), -jnp.sin(ang)], axis=-1)
    hi = lax.Precision.HIGHEST
    h = jnp.sin(freq * (jnp.dot(z, w0, precision=hi) + b0))
    h = jnp.sin(freq * (jnp.dot(h, w1, precision=hi) + b1))
    h = jnp.sin(freq * (jnp.dot(h, w2, precision=hi) + b2))
    h = jnp.dot(h, w3, precision=hi)
    max_decay = math.log(HY_TARGET) / HY_FAST_DECAY
    min_decay = math.log(HY_TARGET) / HY_SLOW_DECAY
    deltas = jnp.abs(jnp.linspace(min_decay, max_decay, d_hy, dtype=F32))
    decay = jnp.exp(-t * deltas[None, :])
    h_fwd = h[:, :d_hy] * decay
    h_bwd = h[:, d_hy:] * decay
    taps = jnp.concatenate([h_fwd, jnp.zeros((1, d_hy), F32), h_bwd[:0:-1]], axis=0)
    taps = taps / jnp.sum(jnp.abs(taps), axis=0, keepdims=True)
    return jnp.fft.rfft(taps, axis=0)


def _spectrum_layout(spec, seq):
    n1 = 2 * seq // DFT_N2
    full = jnp.concatenate([spec, jnp.conj(spec[seq - 1:0:-1])], axis=0)
    full = jnp.transpose(full.reshape(DFT_N2, n1, -1), (1, 0, 2))
    return jnp.stack([jnp.real(full), jnp.imag(full)], axis=0).astype(BF16)


DFT_N2 = 128
DFT_COLS = 2048


def _dft_tables(seq):
    n = 2 * seq
    n2 = DFT_N2
    n1 = n // n2
    k1 = jnp.arange(n1, dtype=jnp.int32)
    m1 = jnp.arange(n1 // 2, dtype=jnp.int32)
    ang1 = (2.0 * math.pi / n1) * ((k1[:, None] * m1[None, :]) % n1).astype(F32)
    f1r, f1i = jnp.cos(ang1), -jnp.sin(ang1)
    fwd_a = jnp.concatenate([f1r, f1i], axis=0).astype(BF16)
    inv_a = (jnp.concatenate([f1r.T, f1i.T], axis=1) * (1.0 / n)).astype(BF16)
    k2 = jnp.arange(n2, dtype=jnp.int32)
    ang2 = (2.0 * math.pi / n2) * ((k2[:, None] * k2[None, :]) % n2).astype(F32)
    f2r, f2i = jnp.cos(ang2), -jnp.sin(ang2)
    fwd_c = jnp.block([[f2r, -f2i], [f2i, f2r]]).astype(BF16)
    inv_c = jnp.block([[f2r, f2i], [-f2i, f2r]]).astype(BF16)
    angt = (2.0 * math.pi / n) * (k1[:, None] * k2[None, :]).astype(F32)
    twr, twi = jnp.cos(angt), -jnp.sin(angt)
    return dict(fwd_a=fwd_a, inv_a=inv_a, fwd_c=fwd_c, inv_c=inv_c,
                tw_by_n2=(twr.T[:, :, None], twi.T[:, :, None]),
                tw_by_k1=(twr[:, :, None], twi[:, :, None]))


HALO = 16


def _hy_short_kernel(x1c, x1p, x1n, x2c, x2p, x2n, vc, vp, vn, w_ref, b_ref, vv_ref, x1o_ref):
    i = pl.program_id(1)
    last = pl.num_programs(1) - 1
    tm = x1c.shape[1]
    row = lax.broadcasted_iota(jnp.int32, (tm, 1), 0)

    def conv(cur_ref, prev_ref, next_ref, g):
        cur = cur_ref[0].astype(F32)
        before = jnp.where(i > 0, prev_ref[0, HALO - 1:HALO, :].astype(F32), 0.0)
        after = jnp.where(i < last, next_ref[0, 0:1, :].astype(F32), 0.0)
        up = jnp.where(row == 0, before, pltpu.roll(cur, 1, axis=0))
        dn = jnp.where(row == tm - 1, after, pltpu.roll(cur, tm - 1, axis=0))
        return (w_ref[0, g:g + 1, :] * up + w_ref[1, g:g + 1, :] * cur + w_ref[2, g:g + 1, :] * dn
                + b_ref[g:g + 1, :])

    x1 = conv(x1c, x1p, x1n, 0)
    x2 = conv(x2c, x2p, x2n, 1)
    v = conv(vc, vp, vn, 2)
    vv_ref[0] = (v * x2).astype(vv_ref.dtype)
    x1o_ref[0] = x1.astype(x1o_ref.dtype)


def _hy_short(proj, conv_w, conv_b, d_hy):
    bsz, seq, _ = proj.shape
    tm = _largest_tile(seq, 1024, HALO)
    cb = _largest_tile(d_hy, 512, LANES)
    ncb = d_hy // cb
    nh = seq // HALO

    def specs(g):
        return [
            pl.BlockSpec((1, tm, cb), lambda b, i, c: (b, i, g * ncb + c)),
            pl.BlockSpec((1, HALO, cb), lambda b, i, c: (b, jnp.maximum(i * (tm // HALO) - 1, 0), g * ncb + c)),
            pl.BlockSpec((1, HALO, cb), lambda b, i, c: (b, jnp.minimum((i + 1) * (tm // HALO), nh - 1),
                                                         g * ncb + c)),
        ]

    out_spec = pl.BlockSpec((1, tm, cb), lambda b, i, c: (b, i, c))
    return pl.pallas_call(
        _hy_short_kernel,
        grid=(bsz, seq // tm, ncb),
        in_specs=specs(0) + specs(1) + specs(2) + [
            pl.BlockSpec((3, 3, cb), lambda b, i, c: (0, 0, c)),
            pl.BlockSpec((3, cb), lambda b, i, c: (0, c)),
        ],
        out_specs=[out_spec, out_spec],
        out_shape=[jax.ShapeDtypeStruct((bsz, seq, d_hy), BF16)] * 2,
        compiler_params=_cparams(("parallel", "parallel", "parallel")),
        name="hy_short",
    )(*([proj] * 9), conv_w.reshape(3, 3, d_hy), conv_b.reshape(3, d_hy))


def _hy_fwd_a_kernel(v_ref, fa_ref, twr_ref, twi_ref, o_ref, *, c):
    n1 = fa_ref.shape[0] // 2
    res = jnp.dot(fa_ref[...], v_ref[0], preferred_element_type=F32)
    for s in range(v_ref.shape[2] // c):
        cols = slice(s * c, (s + 1) * c)
        ar, ai = res[:n1, cols], res[n1:, cols]
        tr, ti = twr_ref[s], twi_ref[s]
        o_ref[0, 0, :, cols] = (ar * tr - ai * ti).astype(o_ref.dtype)
        o_ref[0, 1, :, cols] = (ar * ti + ai * tr).astype(o_ref.dtype)


def _hy_fwd_a(vv, tabs):
    bsz, seq, c = vv.shape
    n2 = DFT_N2
    n1 = 2 * seq // n2
    per = max(1, DFT_COLS // c)
    cols = per * c
    twr, twi = tabs["tw_by_n2"]
    return pl.pallas_call(
        functools.partial(_hy_fwd_a_kernel, c=c),
        grid=(bsz, n2 // per),
        in_specs=[
            pl.BlockSpec((1, n1 // 2, cols), lambda b, j: (b, 0, j)),
            pl.BlockSpec((2 * n1, n1 // 2), lambda b, j: (0, 0)),
            pl.BlockSpec((per, n1, 1), lambda b, j: (j, 0, 0)),
            pl.BlockSpec((per, n1, 1), lambda b, j: (j, 0, 0)),
        ],
        out_specs=pl.BlockSpec((1, 2, n1, cols), lambda b, j: (b, 0, 0, j)),
        out_shape=jax.ShapeDtypeStruct((bsz, 2, n1, n2 * c), BF16),
        compiler_params=_cparams(("parallel", "parallel")),
        name="hy_fwd_a",
    )(vv.reshape(bsz, n1 // 2, n2 * c), tabs["fwd_a"], twr, twi)


K1_PER_STEP = 16


def _hy_mid_kernel(a_ref, h_ref, fc_ref, ic_ref, twr_ref, twi_ref, o_ref):
    n2 = DFT_N2
    for k in range(a_ref.shape[2]):
        a = jnp.concatenate([a_ref[0, 0, k], a_ref[0, 1, k]], axis=0)
        x = jnp.dot(fc_ref[...], a, preferred_element_type=F32)
        xr, xi = x[:n2], x[n2:]
        hr, hi = h_ref[0, k].astype(F32), h_ref[1, k].astype(F32)
        y = jnp.concatenate([xr * hr - xi * hi, xr * hi + xi * hr], axis=0).astype(BF16)
        z = jnp.dot(ic_ref[...], y, preferred_element_type=F32)
        zr, zi = z[:n2], z[n2:]
        tr, ti = twr_ref[k], twi_ref[k]
        o_ref[0, 0, k] = (zr * tr + zi * ti).astype(o_ref.dtype)
        o_ref[0, 1, k] = (zi * tr - zr * ti).astype(o_ref.dtype)


def _hy_mid(a, spec_l, tabs):
    bsz, _, n1, n2c = a.shape
    n2 = DFT_N2
    c = n2c // n2
    cb = _largest_tile(c, 256, LANES)
    kc = _largest_tile(n1, K1_PER_STEP, 1)
    twr, twi = tabs["tw_by_k1"]
    return pl.pallas_call(
        _hy_mid_kernel,
        grid=(c // cb, n1 // kc, bsz),
        in_specs=[
            pl.BlockSpec((1, 2, kc, n2, cb), lambda j, k, b: (b, 0, k, 0, j)),
            pl.BlockSpec((2, kc, n2, cb), lambda j, k, b: (0, k, 0, j)),
            pl.BlockSpec((2 * n2, 2 * n2), lambda j, k, b: (0, 0)),
            pl.BlockSpec((2 * n2, 2 * n2), lambda j, k, b: (0, 0)),
            pl.BlockSpec((kc, n2, 1), lambda j, k, b: (k, 0, 0)),
            pl.BlockSpec((kc, n2, 1), lambda j, k, b: (k, 0, 0)),
        ],
        out_specs=pl.BlockSpec((1, 2, kc, n2, cb), lambda j, k, b: (b, 0, k, 0, j)),
        out_shape=jax.ShapeDtypeStruct((bsz, 2, n1, n2, c), BF16),
        compiler_params=_cparams(("parallel", "parallel", "parallel")),
        name="hy_mid",
    )(a.reshape(bsz, 2, n1, n2, c), spec_l, tabs["fwd_c"], tabs["inv_c"], twr, twi)


def _hy_inv_a_kernel(z_ref, ga_ref, vv_ref, x1_ref, skip_ref, o_ref):
    y = jnp.dot(ga_ref[...], z_ref[0], preferred_element_type=F32)
    v = vv_ref[0].astype(F32)
    o_ref[0] = ((y + skip_ref[...] * v) * x1_ref[0].astype(F32)).astype(o_ref.dtype)


def _hy_inv_a(z, vv, x1c, skip, tabs):
    bsz, seq, c = vv.shape
    n2 = DFT_N2
    n1 = 2 * seq // n2
    per = max(1, DFT_COLS // c)
    cols = per * c
    flat = lambda t: t.reshape(bsz, n1 // 2, n2 * c)
    data = pl.BlockSpec((1, n1 // 2, cols), lambda b, j: (b, 0, j))
    out = pl.pallas_call(
        _hy_inv_a_kernel,
        grid=(bsz, n2 // per),
        in_specs=[
            pl.BlockSpec((1, 2 * n1, cols), lambda b, j: (b, 0, j)),
            pl.BlockSpec((n1 // 2, 2 * n1), lambda b, j: (0, 0)),
            data, data,
            pl.BlockSpec((1, cols), lambda b, j: (0, 0)),
        ],
        out_specs=data,
        out_shape=jax.ShapeDtypeStruct((bsz, n1 // 2, n2 * c), BF16),
        compiler_params=_cparams(("parallel", "parallel")),
        name="hy_inv_a",
    )(z.reshape(bsz, 2 * n1, n2 * c), tabs["inv_a"], flat(vv), flat(x1c), jnp.tile(skip, per).reshape(1, cols))
    return out.reshape(bsz, seq, c)


def _hyena_mixer(proj, conv_w, conv_b, spec_l, skip, tabs, d_hy):
    vv, x1c = _hy_short(proj, conv_w, conv_b, d_hy)
    a = _hy_fwd_a(vv, tabs)
    z = _hy_mid(a, spec_l, tabs)
    return _hy_inv_a(z, vv, x1c, skip, tabs)


def _layer(x, ada, p):
    bsz, seq, d = x.shape
    d_hy = p["hy_skip"].shape[0]
    d_att = p["out_norm_att"].shape[0]
    n_exp = p["w_router"].shape[1]
    sh1, sc1, g1, sh2, sc2, g2 = jnp.split(ada, N_ADA, axis=-1)

    q_gain = p["q_norm_g"] * (HEAD_DIM ** -0.5)
    proj = _inproj(x, p["norm1_g"], sc1, sh1, p["w_in_bf16"], q_gain, p["k_norm_g"], d_hy, d_att)

    y_hy = _hyena_mixer(proj, p["hy_conv_w"], p["hy_conv_b"], p["spec"], p["hy_skip"], p["dft"], d_hy)
    y_at = _natten(proj, p["bias_tab"], d_hy, d_att)

    x1, h2, aff = _outproj(y_hy, y_at, x, p["out_norm_hy"], p["out_norm_att"], p["w_out_bf16"],
                           g1, p["norm2_g"], sc2, sh2, p["w_router"])

    n_tok = bsz * seq
    cap = CAPACITY_FACTOR * n_tok // n_exp
    gates, idx = lax.top_k(aff.reshape(n_tok, n_exp).T, cap)
    xg = jnp.take(h2.reshape(n_tok, d), idx, axis=0)
    ye = _expert_ffn(xg, p["wg"], p["wu"], p["wd"], gates)
    routed = jnp.zeros((n_tok, d), F32).at[idx.reshape(-1)].add(ye.reshape(-1, d).astype(F32))
    return x1 + g2[:, None, :] * routed.reshape(bsz, seq, d)


def kernel(x_prompt, x_sample, c_prompt, c_sample, ada_w, ada_b, norm1_g, w_in, hy_conv_w, hy_conv_b, hy_f_w0, hy_f_b0, hy_f_w1, hy_f_b1, hy_f_w2, hy_f_b2, hy_f_w3, hy_f_freq, hy_skip, q_norm_g, k_norm_g, rpb, out_norm_hy, out_norm_att, w_out, norm2_g, w_router, w_gate, w_up, w_down):
    depth = ada_w.shape[0]
    y_prompt, y_sample = x_prompt, x_sample
    nbp = x_prompt.shape[0]
    assert x_prompt.shape[1] == x_sample.shape[1]
    seq = x_prompt.shape[1]
    for l in range(depth):
        d_hy = hy_skip.shape[-1]
        spec = _hyena_filter_spectrum(seq, hy_f_w0[l], hy_f_b0[l], hy_f_w1[l], hy_f_b1[l], hy_f_w2[l],
                                      hy_f_b2[l], hy_f_w3[l], hy_f_freq[l], d_hy)
        p = {
            "norm1_g": norm1_g[l], "w_in_bf16": w_in[l].astype(BF16),
            "hy_conv_w": hy_conv_w[l], "hy_conv_b": hy_conv_b[l], "hy_skip": hy_skip[l],
            "spec": _spectrum_layout(spec, seq), "dft": _dft_tables(seq),
            "q_norm_g": q_norm_g[l], "k_norm_g": k_norm_g[l], "bias_tab": _natten_bias(rpb[l]),
            "out_norm_hy": out_norm_hy[l], "out_norm_att": out_norm_att[l],
            "w_out_bf16": w_out[l].astype(BF16), "norm2_g": norm2_g[l], "w_router": w_router[l],
            "wg": w_gate[l].astype(BF16), "wu": w_up[l].astype(BF16), "wd": w_down[l].astype(BF16),
        }
        ada = _ada(jnp.concatenate([c_prompt, c_sample], axis=0), ada_w[l], ada_b[l])
        y_prompt = _layer(y_prompt, ada[:nbp], p)
        y_sample = _layer(y_sample, ada[nbp:], p)
    return (y_prompt, y_sample)
```

```python
import functools
import math

import jax
import jax.numpy as jnp
from jax import lax
from jax.experimental import pallas as pl
from jax.experimental.pallas import tpu as pltpu

F32 = jnp.float32
BF16 = jnp.bfloat16

HEAD_DIM = 64
GRID_W = 64
NA_ROWS = 8
NA_COLS = 16
HY_BANDS = 16
HY_TARGET = 1e-2
HY_FAST_DECAY = 0.3
HY_SLOW_DECAY = 1.5
CAPACITY_FACTOR = 2
N_ADA = 6
EPS = 1e-6
NEG_INF = -1e30

LANES = 128
VMEM_LIMIT_BYTES = 56 * 1024 * 1024

ROWS_PER_STEP = 8
TOK_PER_STEP = ROWS_PER_STEP * GRID_W


def _largest_tile(n, pref, align):
    if n <= pref:
        return n
    t = (pref // align) * align
    while t > align and n % t:
        t -= align
    assert n % t == 0, (n, pref, align)
    return t


def _cparams(sem):
    return pltpu.CompilerParams(dimension_semantics=sem, vmem_limit_bytes=VMEM_LIMIT_BYTES)


def _ada_kernel(ct_ref, w_ref, b_ref, o_ref):
    ct = ct_ref[...]
    st = ct * jax.nn.sigmoid(ct)
    w = w_ref[...]
    for b in range(ct.shape[1]):
        o_ref[b:b + 1, :] = jnp.sum(w * st[:, b:b + 1], axis=0, keepdims=True) + b_ref[...]


def _ada(c, ada_w, ada_b):
    nb, d = c.shape
    n = ada_w.shape[1]
    tn = _largest_tile(n, 1024, LANES)
    return pl.pallas_call(
        _ada_kernel,
        grid=(n // tn,),
        in_specs=[
            pl.BlockSpec((d, nb), lambda j: (0, 0)),
            pl.BlockSpec((d, tn), lambda j: (0, j)),
            pl.BlockSpec((1, tn), lambda j: (0, j)),
        ],
        out_specs=pl.BlockSpec((nb, tn), lambda j: (0, j)),
        out_shape=jax.ShapeDtypeStruct((nb, n), F32),
        compiler_params=_cparams(("arbitrary",)),
        name="ada",
    )(c.T, ada_w, ada_b.reshape(1, n))


def _inproj_kernel(x_ref, g_ref, sc_ref, sh_ref, w_ref, qg_ref, kg_ref, bd_ref, o_ref, h_scr,
                   *, tn, q_col0, k_col0, v_col0):
    j = pl.program_id(2)

    @pl.when(j == 0)
    def _():
        x = x_ref[0]
        ms = jnp.mean(x * x, axis=-1, keepdims=True)
        y = (x * lax.rsqrt(ms + EPS)) * g_ref[...]
        h_scr[...] = (y * (1.0 + sc_ref[0]) + sh_ref[0]).astype(BF16)

    acc = jnp.dot(h_scr[...], w_ref[...], preferred_element_type=F32)
    col0 = j * tn
    is_q = jnp.logical_and(col0 >= q_col0, col0 < k_col0)
    is_k = jnp.logical_and(col0 >= k_col0, col0 < v_col0)
    is_qk = jnp.logical_or(is_q, is_k)

    @pl.when(jnp.logical_not(is_qk))
    def _():
        o_ref[0] = acc.astype(o_ref.dtype)

    @pl.when(is_qk)
    def _():
        gain = jnp.where(is_q, qg_ref[...], kg_ref[...])
        bd = bd_ref[...]
        for c in range(tn // LANES):
            sl = slice(c * LANES, (c + 1) * LANES)
            a = acc[:, sl]
            sq = a * a
            hi = sq.astype(BF16)
            lo = (sq - hi.astype(F32)).astype(BF16)
            ss = (jnp.dot(hi, bd, preferred_element_type=F32)
                  + jnp.dot(lo, bd, preferred_element_type=F32))
            y = (a * lax.rsqrt(ss * (1.0 / HEAD_DIM) + EPS)) * gain[:, sl]
            o_ref[0, :, sl] = y.astype(o_ref.dtype)


def _inproj(x, norm_g, sc, sh, w_bf16, q_gain_row, k_gain_row, d_hy, d_att):
    bsz, seq, d = x.shape
    n = w_bf16.shape[1]
    tm = _largest_tile(seq, 1024, 16)
    tn = _largest_tile(d_att, 1024, LANES)
    assert (3 * d_hy) % tn == 0 and n % tn == 0
    lane = jnp.arange(LANES)
    bd = (lane[:, None] // HEAD_DIM == lane[None, :] // HEAD_DIM).astype(BF16)
    kern = functools.partial(_inproj_kernel, tn=tn, q_col0=3 * d_hy, k_col0=3 * d_hy + d_att,
                             v_col0=3 * d_hy + 2 * d_att)
    return pl.pallas_call(
        kern,
        grid=(bsz, seq // tm, n // tn),
        in_specs=[
            pl.BlockSpec((1, tm, d), lambda b, i, j: (b, i, 0)),
            pl.BlockSpec((1, d), lambda b, i, j: (0, 0)),
            pl.BlockSpec((1, 1, d), lambda b, i, j: (b, 0, 0)),
            pl.BlockSpec((1, 1, d), lambda b, i, j: (b, 0, 0)),
            pl.BlockSpec((d, tn), lambda b, i, j: (0, j)),
            pl.BlockSpec((1, tn), lambda b, i, j: (0, 0)),
            pl.BlockSpec((1, tn), lambda b, i, j: (0, 0)),
            pl.BlockSpec((LANES, LANES), lambda b, i, j: (0, 0)),
        ],
        out_specs=pl.BlockSpec((1, tm, tn), lambda b, i, j: (b, i, j)),
        out_shape=jax.ShapeDtypeStruct((bsz, seq, n), BF16),
        scratch_shapes=[pltpu.VMEM((tm, d), BF16)],
        compiler_params=_cparams(("parallel", "parallel", "arbitrary")),
        name="inproj",
    )(x, norm_g.reshape(1, d), sc[:, None, :], sh[:, None, :], w_bf16,
      jnp.tile(q_gain_row, tn // HEAD_DIM).reshape(1, tn),
      jnp.tile(k_gain_row, tn // HEAD_DIM).reshape(1, tn), bd)


def _natten_rows(q_ref, k_refs, v_refs, bias_ref, o_ref, first_row):
    first_head = lax.broadcasted_iota(jnp.int32, (1, LANES), 1) < HEAD_DIM
    zero = jnp.zeros((), BF16)

    def window(refs, row):
        blk, o = divmod(row, ROWS_PER_STEP)
        if o == 0:
            return refs[blk][0]
        return jnp.concatenate([refs[blk][0, o * GRID_W:, :], refs[blk + 1][0, :o * GRID_W, :]], axis=0)

    for rr in range(ROWS_PER_STEP):
        row = first_row[rr]
        start = row - ROWS_PER_STEP - rr + NA_ROWS - 1
        kw = window(k_refs, row)
        vw = window(v_refs, row)
        q = q_ref[0, rr * GRID_W:(rr + 1) * GRID_W, :]
        q2 = jnp.concatenate([jnp.where(first_head, q, zero), jnp.where(first_head, zero, q)], axis=0)
        s = lax.dot_general(q2, kw, (((1,), (1,)), ((), ())), preferred_element_type=F32)
        s = s + jnp.concatenate([bias_ref[0, 0, start], bias_ref[0, 1, start]], axis=0)
        m = jnp.max(s, axis=-1, keepdims=True)
        p = jnp.exp(s - m)
        l = jnp.sum(p, axis=-1, keepdims=True)
        o2 = jnp.dot(p.astype(BF16), vw, preferred_element_type=F32) / l
        o = jnp.where(first_head, o2[:GRID_W], o2[GRID_W:])
        o_ref[0, rr * GRID_W:(rr + 1) * GRID_W, :] = o.astype(o_ref.dtype)


def _natten_kernel(q_ref, k0_ref, k1_ref, k2_ref, v0_ref, v1_ref, v2_ref, bias_ref, o_ref):
    i = pl.program_id(2)
    last = pl.num_programs(2) - 1
    k_refs, v_refs = (k0_ref, k1_ref, k2_ref), (v0_ref, v1_ref, v2_ref)
    half = NA_ROWS // 2
    cases = (
        (i == 0, [ROWS_PER_STEP + max(rr - half, 0) for rr in range(ROWS_PER_STEP)]),
        (i == last, [min(rr + ROWS_PER_STEP - half, ROWS_PER_STEP) for rr in range(ROWS_PER_STEP)]),
        (jnp.logical_and(i > 0, i < last), [rr + ROWS_PER_STEP - half for rr in range(ROWS_PER_STEP)]),
    )
    for cond, first_row in cases:
        @pl.when(cond)
        def _(first_row=first_row):
            _natten_rows(q_ref, k_refs, v_refs, bias_ref, o_ref, first_row)


def _natten_bias(rpb):
    h = rpb.shape[0]
    cols = jnp.arange(GRID_W)
    c0 = jnp.clip(cols - NA_COLS // 2, 0, GRID_W - NA_COLS)
    col_mask = (cols[None, :] >= c0[:, None]) & (cols[None, :] < c0[:, None] + NA_COLS)
    dc = jnp.clip(cols[None, :] - cols[:, None], -(NA_COLS - 1), NA_COLS - 1) + NA_COLS - 1
    rpb_cols = rpb.astype(F32)[:, :, dc]
    dr = jnp.arange(NA_ROWS)[:, None] + jnp.arange(NA_ROWS)[None, :]
    tab = rpb_cols[:, dr]
    tab = jnp.where(col_mask[None, None, None], tab, NEG_INF)
    tab = jnp.transpose(tab, (0, 1, 3, 2, 4)).reshape(h // 2, 2, NA_ROWS, GRID_W, NA_ROWS * GRID_W)
    return tab


def _natten(proj, bias_tab, d_hy, d_att):
    bsz, seq, _ = proj.shape
    rows = seq // GRID_W
    nblk = rows // ROWS_PER_STEP
    assert rows % ROWS_PER_STEP == 0 and rows >= NA_ROWS and nblk >= 2
    hp = d_att // LANES
    qc, kc, vc = (3 * d_hy) // LANES, (3 * d_hy + d_att) // LANES, (3 * d_hy + 2 * d_att) // LANES
    t = TOK_PER_STEP

    def blk(col, shift):
        return pl.BlockSpec(
            (1, t, LANES),
            lambda b, p, i: (b, jnp.clip(i + shift, 0, nblk - 1), col + p))

    return pl.pallas_call(
        _natten_kernel,
        grid=(bsz, hp, nblk),
        in_specs=[blk(qc, 0), blk(kc, -1), blk(kc, 0), blk(kc, 1), blk(vc, -1), blk(vc, 0), blk(vc, 1),
                  pl.BlockSpec((1, 2, NA_ROWS, GRID_W, NA_ROWS * GRID_W), lambda b, p, i: (p, 0, 0, 0, 0))],
        out_specs=pl.BlockSpec((1, t, LANES), lambda b, p, i: (b, i, p)),
        out_shape=jax.ShapeDtypeStruct((bsz, seq, d_att), BF16),
        compiler_params=_cparams(("parallel", "parallel", "arbitrary")),
        name="natten",
    )(proj, proj, proj, proj, proj, proj, proj, bias_tab)


def _outproj_kernel(yh_ref, ya_ref, x_ref, gh_ref, ga_ref, wh_ref, wa_ref, g1_ref, n2_ref, sc_ref, sh_ref,
                    wr_ref, x1_ref, h2_ref, aff_ref):
    def _norm(y_ref, g_ref):
        y = y_ref[0].astype(F32)
        ms = jnp.mean(y * y, axis=-1, keepdims=True)
        return ((y * lax.rsqrt(ms + EPS)) * g_ref[...]).astype(BF16)

    mixed = (jnp.dot(_norm(yh_ref, gh_ref), wh_ref[...], preferred_element_type=F32)
             + jnp.dot(_norm(ya_ref, ga_ref), wa_ref[...], preferred_element_type=F32))
    x1 = x_ref[0] + g1_ref[0] * mixed
    x1_ref[0] = x1
    ms = jnp.mean(x1 * x1, axis=-1, keepdims=True)
    h2 = ((x1 * lax.rsqrt(ms + EPS)) * n2_ref[...]) * (1.0 + sc_ref[0]) + sh_ref[0]
    h2_ref[0] = h2.astype(BF16)
    wr = wr_ref[...]
    w_hi = wr.astype(BF16)
    w_lo = (wr - w_hi.astype(F32)).astype(BF16)
    h_hi = h2.astype(BF16)
    h_lo = (h2 - h_hi.astype(F32)).astype(BF16)
    logits = (jnp.dot(h_hi, w_hi, preferred_element_type=F32)
              + jnp.dot(h_lo, w_hi, preferred_element_type=F32)
              + jnp.dot(h_hi, w_lo, preferred_element_type=F32))
    m = jnp.max(logits, axis=-1, keepdims=True)
    e = jnp.exp(logits - m)
    aff_ref[0] = e / jnp.sum(e, axis=-1, keepdims=True)


def _outproj(y_hy, y_at, x, gain_hy, gain_att, w_out_bf16, g1, norm2_g, sc2, sh2, w_router):
    bsz, seq, d = x.shape
    d_hy, d_att = y_hy.shape[-1], y_at.shape[-1]
    n_exp = w_router.shape[1]
    tm = _largest_tile(seq, 512, 16)
    row = lambda v: v[:, None, :]
    per_b = pl.BlockSpec((1, 1, d), lambda b, i: (b, 0, 0))
    const = lambda shape: pl.BlockSpec(shape, lambda b, i: tuple(0 for _ in shape))
    return pl.pallas_call(
        _outproj_kernel,
        grid=(bsz, seq // tm),
        in_specs=[
            pl.BlockSpec((1, tm, d_hy), lambda b, i: (b, i, 0)),
            pl.BlockSpec((1, tm, d_att), lambda b, i: (b, i, 0)),
            pl.BlockSpec((1, tm, d), lambda b, i: (b, i, 0)),
            const((1, d_hy)), const((1, d_att)),
            const((d_hy, d)), const((d_att, d)),
            per_b, const((1, d)), per_b, per_b,
            const((d, n_exp)),
        ],
        out_specs=[
            pl.BlockSpec((1, tm, d), lambda b, i: (b, i, 0)),
            pl.BlockSpec((1, tm, d), lambda b, i: (b, i, 0)),
            pl.BlockSpec((1, tm, n_exp), lambda b, i: (b, i, 0)),
        ],
        out_shape=[
            jax.ShapeDtypeStruct((bsz, seq, d), F32),
            jax.ShapeDtypeStruct((bsz, seq, d), BF16),
            jax.ShapeDtypeStruct((bsz, seq, n_exp), F32),
        ],
        compiler_params=_cparams(("parallel", "arbitrary")),
        name="outproj",
    )(y_hy, y_at, x, gain_hy.reshape(1, d_hy), gain_att.reshape(1, d_att),
      w_out_bf16[:d_hy], w_out_bf16[d_hy:], row(g1), norm2_g.reshape(1, d), row(sc2), row(sh2), w_router)


FF_TILE = 512
OUT_TILE = 512


def _ffn_kernel(x_ref, wg_ref, wu_ref, wd_ref, gt_ref, o_ref, he_ref, *, ff, nf):
    f = pl.program_id(2)

    @pl.when(f < nf)
    def _():
        x = x_ref[0]
        g = jnp.dot(x, wg_ref[0], preferred_element_type=F32)
        u = jnp.dot(x, wu_ref[0], preferred_element_type=F32)
        he = ((g * jax.nn.sigmoid(g)) * u).astype(BF16)
        he_ref[:, pl.ds(pl.multiple_of(f * FF_TILE, FF_TILE), FF_TILE)] = he

    @pl.when(f >= nf)
    def _():
        y = jnp.dot(he_ref[:, :ff], wd_ref[0], preferred_element_type=F32)
        o_ref[0] = (y * gt_ref[0]).astype(o_ref.dtype)


def _expert_ffn(xg, wg, wu, wd, gates):
    n_exp, cap, d = xg.shape
    ff = wg.shape[2]
    tm = _largest_tile(cap, 1024, 16)
    tn = _largest_tile(d, OUT_TILE, LANES)
    nf = pl.cdiv(ff, FF_TILE)
    up_idx = lambda e, m, f: (e, 0, jnp.minimum(f, nf - 1))
    out_idx = lambda e, m, f: (e, m, jnp.maximum(f - nf, 0))
    return pl.pallas_call(
        functools.partial(_ffn_kernel, ff=ff, nf=nf),
        grid=(n_exp, cap // tm, nf + d // tn),
        in_specs=[
            pl.BlockSpec((1, tm, d), lambda e, m, f: (e, m, 0)),
            pl.BlockSpec((1, d, FF_TILE), up_idx),
            pl.BlockSpec((1, d, FF_TILE), up_idx),
            pl.BlockSpec((1, ff, tn), lambda e, m, f: (e, 0, jnp.maximum(f - nf, 0))),
            pl.BlockSpec((1, tm, 1), lambda e, m, f: (e, m, 0)),
        ],
        out_specs=pl.BlockSpec((1, tm, tn), out_idx),
        out_shape=jax.ShapeDtypeStruct((n_exp, cap, d), BF16),
        scratch_shapes=[pltpu.VMEM((tm, nf * FF_TILE), BF16)],
        compiler_params=_cparams(("parallel", "parallel", "arbitrary")),
        name="expert_ffn",
    )(xg, wg, wu, wd, gates[:, :, None])


def _hyena_filter_taps(seq, w0, b0, w1, b1, w2, b2, w3, freq, d_hy):
    pos = jnp.arange(seq, dtype=F32)
    t = jnp.linspace(0.0, 1.0, seq, dtype=F32)[:, None]
    bands = jnp.linspace(1e-4, HY_BANDS - 1, HY_BANDS, dtype=F32)
    ang = (2.0 * math.pi / seq) * pos[:, None] * bands[None, :]
    z = jnp.concatenate([t, jnp.cos(ang), -jnp.sin(ang)], axis=-1)
    hi = lax.Precision.HIGHEST
    h = jnp.sin(freq * (jnp.dot(z, w0, precision=hi) + b0))
    h = jnp.sin(freq * (jnp.dot(h, w1, precision=hi) + b1))
    h = jnp.sin(freq * (jnp.dot(h, w2, precision=hi) + b2))
    h = jnp.dot(h, w3, precision=hi)
    max_decay = math.log(HY_TARGET) / HY_FAST_DECAY
    min_decay = math.log(HY_TARGET) / HY_SLOW_DECAY
    deltas = jnp.abs(jnp.linspace(min_decay, max_decay, d_hy, dtype=F32))
    decay = jnp.exp(-t * deltas[None, :])
    h_fwd = h[:, :d_hy] * decay
    h_bwd = jnp.where(pos[:, None] > 0, h[:, d_hy:] * decay, 0.0)
    norm = jnp.sum(jnp.abs(h_fwd) + jnp.abs(h_bwd), axis=0, keepdims=True)
    return (jnp.stack([h_fwd, h_bwd], axis=0) / norm).astype(BF16)


DFT_N2 = 128
DFT_COLS = 2048


def _dft_tables(seq):
    n = 2 * seq
    n2 = DFT_N2
    n1 = n // n2
    k1 = jnp.arange(n1, dtype=jnp.int32)
    m1 = jnp.arange(n1 // 2, dtype=jnp.int32)
    ang1 = (2.0 * math.pi / n1) * ((k1[:, None] * m1[None, :]) % n1).astype(F32)
    f1r, f1i = jnp.cos(ang1), -jnp.sin(ang1)
    fwd_a = jnp.concatenate([f1r, f1i], axis=0).astype(BF16)
    inv_a = (jnp.concatenate([f1r.T, f1i.T], axis=1) * (1.0 / n)).astype(BF16)
    k2 = jnp.arange(n2, dtype=jnp.int32)
    ang2 = (2.0 * math.pi / n2) * ((k2[:, None] * k2[None, :]) % n2).astype(F32)
    f2r, f2i = jnp.cos(ang2), -jnp.sin(ang2)
    fwd_c = jnp.block([[f2r, -f2i], [f2i, f2r]]).astype(BF16)
    inv_c = jnp.block([[f2r, f2i], [-f2i, f2r]]).astype(BF16)
    angt = (2.0 * math.pi / n) * (k1[:, None] * k2[None, :]).astype(F32)
    twr, twi = jnp.cos(angt), -jnp.sin(angt)
    return dict(fwd_a=fwd_a, inv_a=inv_a, fwd_c=fwd_c, inv_c=inv_c,
                tw_by_n2=(twr.T[:, :, None], twi.T[:, :, None]),
                tw_by_k1=(twr[:, :, None], twi[:, :, None]))


HALO = 16


def _hy_short_kernel(x1c, x1p, x1n, x2c, x2p, x2n, vc, vp, vn, w_ref, b_ref, vv_ref, x1o_ref):
    i = pl.program_id(1)
    last = pl.num_programs(1) - 1
    tm = x1c.shape[1]
    row = lax.broadcasted_iota(jnp.int32, (tm, 1), 0)

    def conv(cur_ref, prev_ref, next_ref, g):
        cur = cur_ref[0].astype(F32)
        before = jnp.where(i > 0, prev_ref[0, HALO - 1:HALO, :].astype(F32), 0.0)
        after = jnp.where(i < last, next_ref[0, 0:1, :].astype(F32), 0.0)
        up = jnp.where(row == 0, before, pltpu.roll(cur, 1, axis=0))
        dn = jnp.where(row == tm - 1, after, pltpu.roll(cur, tm - 1, axis=0))
        return (w_ref[0, g:g + 1, :] * up + w_ref[1, g:g + 1, :] * cur + w_ref[2, g:g + 1, :] * dn
                + b_ref[g:g + 1, :])

    x1 = conv(x1c, x1p, x1n, 0)
    x2 = conv(x2c, x2p, x2n, 1)
    v = conv(vc, vp, vn, 2)
    vv_ref[0] = (v * x2).astype(vv_ref.dtype)
    x1o_ref[0] = x1.astype(x1o_ref.dtype)


def _hy_short(proj, conv_w, conv_b, d_hy):
    bsz, seq, _ = proj.shape
    tm = _largest_tile(seq, 1024, HALO)
    cb = _largest_tile(d_hy, 512, LANES)
    ncb = d_hy // cb
    nh = seq // HALO

    def specs(g):
        return [
            pl.BlockSpec((1, tm, cb), lambda b, i, c: (b, i, g * ncb + c)),
            pl.BlockSpec((1, HALO, cb), lambda b, i, c: (b, jnp.maximum(i * (tm // HALO) - 1, 0), g * ncb + c)),
            pl.BlockSpec((1, HALO, cb), lambda b, i, c: (b, jnp.minimum((i + 1) * (tm // HALO), nh - 1),
                                                         g * ncb + c)),
        ]

    out_spec = pl.BlockSpec((1, tm, cb), lambda b, i, c: (b, i, c))
    return pl.pallas_call(
        _hy_short_kernel,
        grid=(bsz, seq // tm, ncb),
        in_specs=specs(0) + specs(1) + specs(2) + [
            pl.BlockSpec((3, 3, cb), lambda b, i, c: (0, 0, c)),
            pl.BlockSpec((3, cb), lambda b, i, c: (0, c)),
        ],
        out_specs=[out_spec, out_spec],
        out_shape=[jax.ShapeDtypeStruct((bsz, seq, d_hy), BF16)] * 2,
        compiler_params=_cparams(("parallel", "parallel", "parallel")),
        name="hy_short",
    )(*([proj] * 9), conv_w.reshape(3, 3, d_hy), conv_b.reshape(3, d_hy))


def _hy_fwd_a_kernel(v_ref, fa_ref, twr_ref, twi_ref, o_ref, *, c):
    n1 = fa_ref.shape[0] // 2
    res = jnp.dot(fa_ref[...], v_ref[0], preferred_element_type=F32)
    for s in range(v_ref.shape[2] // c):
        cols = slice(s * c, (s + 1) * c)
        ar, ai = res[:n1, cols], res[n1:, cols]
        tr, ti = twr_ref[s], twi_ref[s]
        o_ref[0, 0, :, cols] = (ar * tr - ai * ti).astype(o_ref.dtype)
        o_ref[0, 1, :, cols] = (ar * ti + ai * tr).astype(o_ref.dtype)


def _hy_fwd_a(vv, tabs):
    bsz, seq, c = vv.shape
    n2 = DFT_N2
    n1 = 2 * seq // n2
    per = max(1, DFT_COLS // c)
    cols = per * c
    twr, twi = tabs["tw_by_n2"]
    return pl.pallas_call(
        functools.partial(_hy_fwd_a_kernel, c=c),
        grid=(bsz, n2 // per),
        in_specs=[
            pl.BlockSpec((1, n1 // 2, cols), lambda b, j: (b, 0, j)),
            pl.BlockSpec((2 * n1, n1 // 2), lambda b, j: (0, 0)),
            pl.BlockSpec((per, n1, 1), lambda b, j: (j, 0, 0)),
            pl.BlockSpec((per, n1, 1), lambda b, j: (j, 0, 0)),
        ],
        out_specs=pl.BlockSpec((1, 2, n1, cols), lambda b, j: (b, 0, 0, j)),
        out_shape=jax.ShapeDtypeStruct((bsz, 2, n1, n2 * c), BF16),
        compiler_params=_cparams(("parallel", "parallel")),
        name="hy_fwd_a",
    )(vv.reshape(bsz, n1 // 2, n2 * c), tabs["fwd_a"], twr, twi)


K1_PER_STEP = 16


def _hy_mid_kernel(a_ref, h_ref, fc_ref, ic_ref, twr_ref, twi_ref, o_ref):
    n2 = DFT_N2
    for k in range(a_ref.shape[2]):
        a = jnp.concatenate([a_ref[0, 0, k], a_ref[0, 1, k]], axis=0)
        x = jnp.dot(fc_ref[...], a, preferred_element_type=F32)
        xr, xi = x[:n2], x[n2:]
        hr, hi = h_ref[0, k].astype(F32), h_ref[1, k].astype(F32)
        y = jnp.concatenate([xr * hr - xi * hi, xr * hi + xi * hr], axis=0).astype(BF16)
        z = jnp.dot(ic_ref[...], y, preferred_element_type=F32)
        zr, zi = z[:n2], z[n2:]
        tr, ti = twr_ref[k], twi_ref[k]
        o_ref[0, 0, k] = (zr * tr + zi * ti).astype(o_ref.dtype)
        o_ref[0, 1, k] = (zi * tr - zr * ti).astype(o_ref.dtype)


def _hy_mid(a, spec_l, tabs):
    bsz, _, n1, n2c = a.shape
    n2 = DFT_N2
    c = n2c // n2
    cb = _largest_tile(c, 256, LANES)
    kc = _largest_tile(n1, K1_PER_STEP, 1)
    twr, twi = tabs["tw_by_k1"]
    return pl.pallas_call(
        _hy_mid_kernel,
        grid=(c // cb, n1 // kc, bsz),
        in_specs=[
            pl.BlockSpec((1, 2, kc, n2, cb), lambda j, k, b: (b, 0, k, 0, j)),
            pl.BlockSpec((2, kc, n2, cb), lambda j, k, b: (0, k, 0, j)),
            pl.BlockSpec((2 * n2, 2 * n2), lambda j, k, b: (0, 0)),
            pl.BlockSpec((2 * n2, 2 * n2), lambda j, k, b: (0, 0)),
            pl.BlockSpec((kc, n2, 1), lambda j, k, b: (k, 0, 0)),
            pl.BlockSpec((kc, n2, 1), lambda j, k, b: (k, 0, 0)),
        ],
        out_specs=pl.BlockSpec((1, 2, kc, n2, cb), lambda j, k, b: (b, 0, k, 0, j)),
        out_shape=jax.ShapeDtypeStruct((bsz, 2, n1, n2, c), BF16),
        compiler_params=_cparams(("parallel", "parallel", "parallel")),
        name="hy_mid",
    )(a.reshape(bsz, 2, n1, n2, c), spec_l, tabs["fwd_c"], tabs["inv_c"], twr, twi)


def _hy_spec_kernel(a_ref, fc_ref, o_ref):
    n2 = DFT_N2
    cb = a_ref.shape[-1]
    for k in range(a_ref.shape[2]):
        fwd = jnp.concatenate([a_ref[0, 0, k], a_ref[0, 1, k]], axis=0)
        bwd = jnp.concatenate([a_ref[1, 0, k], a_ref[1, 1, k]], axis=0)
        x = jnp.dot(fc_ref[...], jnp.concatenate([fwd, bwd], axis=1), preferred_element_type=F32)
        o_ref[0, k] = (x[:n2, :cb] + x[:n2, cb:]).astype(o_ref.dtype)
        o_ref[1, k] = (x[n2:, :cb] - x[n2:, cb:]).astype(o_ref.dtype)


def _hy_spectrum(taps, tabs):
    a = _hy_fwd_a(taps, tabs)
    _, _, n1, n2c = a.shape
    n2 = DFT_N2
    c = n2c // n2
    cb = _largest_tile(c, 256, LANES)
    kc = _largest_tile(n1, K1_PER_STEP, 1)
    return pl.pallas_call(
        _hy_spec_kernel,
        grid=(c // cb, n1 // kc),
        in_specs=[
            pl.BlockSpec((2, 2, kc, n2, cb), lambda j, k: (0, 0, k, 0, j)),
            pl.BlockSpec((2 * n2, 2 * n2), lambda j, k: (0, 0)),
        ],
        out_specs=pl.BlockSpec((2, kc, n2, cb), lambda j, k: (0, k, 0, j)),
        out_shape=jax.ShapeDtypeStruct((2, n1, n2, c), BF16),
        compiler_params=_cparams(("parallel", "parallel")),
        name="hy_spectrum",
    )(a.reshape(2, 2, n1, n2, c), tabs["fwd_c"])


def _hy_inv_a_kernel(z_ref, ga_ref, vv_ref, x1_ref, skip_ref, o_ref):
    y = jnp.dot(ga_ref[...], z_ref[0], preferred_element_type=F32)
    v = vv_ref[0].astype(F32)
    o_ref[0] = ((y + skip_ref[...] * v) * x1_ref[0].astype(F32)).astype(o_ref.dtype)


def _hy_inv_a(z, vv, x1c, skip, tabs):
    bsz, seq, c = vv.shape
    n2 = DFT_N2
    n1 = 2 * seq // n2
    per = max(1, DFT_COLS // c)
    cols = per * c
    flat = lambda t: t.reshape(bsz, n1 // 2, n2 * c)
    data = pl.BlockSpec((1, n1 // 2, cols), lambda b, j: (b, 0, j))
    out = pl.pallas_call(
        _hy_inv_a_kernel,
        grid=(bsz, n2 // per),
        in_specs=[
            pl.BlockSpec((1, 2 * n1, cols), lambda b, j: (b, 0, j)),
            pl.BlockSpec((n1 // 2, 2 * n1), lambda b, j: (0, 0)),
            data, data,
            pl.BlockSpec((1, cols), lambda b, j: (0, 0)),
        ],
        out_specs=data,
        out_shape=jax.ShapeDtypeStruct((bsz, n1 // 2, n2 * c), BF16),
        compiler_params=_cparams(("parallel", "parallel")),
        name="hy_inv_a",
    )(z.reshape(bsz, 2 * n1, n2 * c), tabs["inv_a"], flat(vv), flat(x1c), jnp.tile(skip, per).reshape(1, cols))
    return out.reshape(bsz, seq, c)


def _hyena_mixer(proj, conv_w, conv_b, spec_l, skip, tabs, d_hy):
    vv, x1c = _hy_short(proj, conv_w, conv_b, d_hy)
    a = _hy_fwd_a(vv, tabs)
    z = _hy_mid(a, spec_l, tabs)
    return _hy_inv_a(z, vv, x1c, skip, tabs)


def _layer(x, ada, p):
    bsz, seq, d = x.shape
    d_hy = p["hy_skip"].shape[0]
    d_att = p["out_norm_att"].shape[0]
    n_exp = p["w_router"].shape[1]
    sh1, sc1, g1, sh2, sc2, g2 = jnp.split(ada, N_ADA, axis=-1)

    q_gain = p["q_norm_g"] * (HEAD_DIM ** -0.5)
    proj = _inproj(x, p["norm1_g"], sc1, sh1, p["w_in_bf16"], q_gain, p["k_norm_g"], d_hy, d_att)

    y_hy = _hyena_mixer(proj, p["hy_conv_w"], p["hy_conv_b"], p["spec"], p["hy_skip"], p["dft"], d_hy)
    y_at = _natten(proj, p["bias_tab"], d_hy, d_att)

    x1, h2, aff = _outproj(y_hy, y_at, x, p["out_norm_hy"], p["out_norm_att"], p["w_out_bf16"],
                           g1, p["norm2_g"], sc2, sh2, p["w_router"])

    n_tok = bsz * seq
    cap = CAPACITY_FACTOR * n_tok // n_exp
    gates, idx = lax.top_k(aff.reshape(n_tok, n_exp).T, cap)
    xg = jnp.take(h2.reshape(n_tok, d), idx, axis=0)
    ye = _expert_ffn(xg, p["wg"], p["wu"], p["wd"], gates)
    routed = jnp.zeros((n_tok, d), F32).at[idx.reshape(-1)].add(ye.reshape(-1, d).astype(F32))
    return x1 + g2[:, None, :] * routed.reshape(bsz, seq, d)


def kernel(x_prompt, x_sample, c_prompt, c_sample, ada_w, ada_b, norm1_g, w_in, hy_conv_w, hy_conv_b, hy_f_w0, hy_f_b0, hy_f_w1, hy_f_b1, hy_f_w2, hy_f_b2, hy_f_w3, hy_f_freq, hy_skip, q_norm_g, k_norm_g, rpb, out_norm_hy, out_norm_att, w_out, norm2_g, w_router, w_gate, w_up, w_down):
    depth = ada_w.shape[0]
    y_prompt, y_sample = x_prompt, x_sample
    nbp = x_prompt.shape[0]
    assert x_prompt.shape[1] == x_sample.shape[1]
    seq = x_prompt.shape[1]
    for l in range(depth):
        d_hy = hy_skip.shape[-1]
        dft = _dft_tables(seq)
        taps = _hyena_filter_taps(seq, hy_f_w0[l], hy_f_b0[l], hy_f_w1[l], hy_f_b1[l], hy_f_w2[l],
                                  hy_f_b2[l], hy_f_w3[l], hy_f_freq[l], d_hy)
        p = {
            "norm1_g": norm1_g[l], "w_in_bf16": w_in[l].astype(BF16),
            "hy_conv_w": hy_conv_w[l], "hy_conv_b": hy_conv_b[l], "hy_skip": hy_skip[l],
            "spec": _hy_spectrum(taps, dft), "dft": dft,
            "q_norm_g": q_norm_g[l], "k_norm_g": k_norm_g[l], "bias_tab": _natten_bias(rpb[l]),
            "out_norm_hy": out_norm_hy[l], "out_norm_att": out_norm_att[l],
            "w_out_bf16": w_out[l].astype(BF16), "norm2_g": norm2_g[l], "w_router": w_router[l],
            "wg": w_gate[l].astype(BF16), "wu": w_up[l].astype(BF16), "wd": w_down[l].astype(BF16),
        }
        ada = _ada(jnp.concatenate([c_prompt, c_sample], axis=0), ada_w[l], ada_b[l])
        y_prompt = _layer(y_prompt, ada[:nbp], p)
        y_sample = _layer(y_sample, ada[nbp:], p)
    return (y_prompt, y_sample)
```

```python
import functools
import math

import jax
import jax.numpy as jnp
from jax import lax
from jax.experimental import pallas as pl
from jax.experimental.pallas import tpu as pltpu

F32 = jnp.float32
BF16 = jnp.bfloat16

HEAD_DIM = 64
GRID_W = 64
NA_ROWS = 8
NA_COLS = 16
HY_BANDS = 16
HY_TARGET = 1e-2
HY_FAST_DECAY = 0.3
HY_SLOW_DECAY = 1.5
CAPACITY_FACTOR = 2
N_ADA = 6
EPS = 1e-6
NEG_INF = -1e30

LANES = 128
VMEM_LIMIT_BYTES = 56 * 1024 * 1024

ROWS_PER_STEP = 8
TOK_PER_STEP = ROWS_PER_STEP * GRID_W


def _largest_tile(n, pref, align):
    if n <= pref:
        return n
    t = (pref // align) * align
    while t > align and n % t:
        t -= align
    assert n % t == 0, (n, pref, align)
    return t


def _cparams(sem):
    return pltpu.CompilerParams(dimension_semantics=sem, vmem_limit_bytes=VMEM_LIMIT_BYTES)


def _ada_kernel(ct_ref, w_ref, b_ref, o_ref):
    ct = ct_ref[...]
    st = ct * jax.nn.sigmoid(ct)
    w = w_ref[...]
    for b in range(ct.shape[1]):
        o_ref[b:b + 1, :] = jnp.sum(w * st[:, b:b + 1], axis=0, keepdims=True) + b_ref[...]


def _ada(c, ada_w, ada_b):
    nb, d = c.shape
    n = ada_w.shape[1]
    tn = _largest_tile(n, 1024, LANES)
    return pl.pallas_call(
        _ada_kernel,
        grid=(n // tn,),
        in_specs=[
            pl.BlockSpec((d, nb), lambda j: (0, 0)),
            pl.BlockSpec((d, tn), lambda j: (0, j)),
            pl.BlockSpec((1, tn), lambda j: (0, j)),
        ],
        out_specs=pl.BlockSpec((nb, tn), lambda j: (0, j)),
        out_shape=jax.ShapeDtypeStruct((nb, n), F32),
        compiler_params=_cparams(("arbitrary",)),
        name="ada",
    )(c.T, ada_w, ada_b.reshape(1, n))


def _inproj_kernel(x_ref, g_ref, sc_ref, sh_ref, w_ref, qg_ref, kg_ref, bd_ref, o_ref, h_scr,
                   *, tn, q_col0, k_col0, v_col0):
    j = pl.program_id(2)

    @pl.when(j == 0)
    def _():
        x = x_ref[0]
        ms = jnp.mean(x * x, axis=-1, keepdims=True)
        y = (x * lax.rsqrt(ms + EPS)) * g_ref[...]
        h_scr[...] = (y * (1.0 + sc_ref[0]) + sh_ref[0]).astype(BF16)

    acc = jnp.dot(h_scr[...], w_ref[...], preferred_element_type=F32)
    col0 = j * tn
    is_q = jnp.logical_and(col0 >= q_col0, col0 < k_col0)
    is_k = jnp.logical_and(col0 >= k_col0, col0 < v_col0)
    is_qk = jnp.logical_or(is_q, is_k)

    @pl.when(jnp.logical_not(is_qk))
    def _():
        o_ref[0] = acc.astype(o_ref.dtype)

    @pl.when(is_qk)
    def _():
        gain = jnp.where(is_q, qg_ref[...], kg_ref[...])
        bd = bd_ref[...]
        for c in range(tn // LANES):
            sl = slice(c * LANES, (c + 1) * LANES)
            a = acc[:, sl]
            sq = a * a
            hi = sq.astype(BF16)
            lo = (sq - hi.astype(F32)).astype(BF16)
            ss = (jnp.dot(hi, bd, preferred_element_type=F32)
                  + jnp.dot(lo, bd, preferred_element_type=F32))
            y = (a * lax.rsqrt(ss * (1.0 / HEAD_DIM) + EPS)) * gain[:, sl]
            o_ref[0, :, sl] = y.astype(o_ref.dtype)


def _inproj(x, norm_g, sc, sh, w_bf16, q_gain_row, k_gain_row, d_hy, d_att):
    bsz, seq, d = x.shape
    n = w_bf16.shape[1]
    tm = _largest_tile(seq, 1024, 16)
    tn = _largest_tile(d_att, 1024, LANES)
    assert (3 * d_hy) % tn == 0 and n % tn == 0
    lane = jnp.arange(LANES)
    bd = (lane[:, None] // HEAD_DIM == lane[None, :] // HEAD_DIM).astype(BF16)
    kern = functools.partial(_inproj_kernel, tn=tn, q_col0=3 * d_hy, k_col0=3 * d_hy + d_att,
                             v_col0=3 * d_hy + 2 * d_att)
    return pl.pallas_call(
        kern,
        grid=(bsz, seq // tm, n // tn),
        in_specs=[
            pl.BlockSpec((1, tm, d), lambda b, i, j: (b, i, 0)),
            pl.BlockSpec((1, d), lambda b, i, j: (0, 0)),
            pl.BlockSpec((1, 1, d), lambda b, i, j: (b, 0, 0)),
            pl.BlockSpec((1, 1, d), lambda b, i, j: (b, 0, 0)),
            pl.BlockSpec((d, tn), lambda b, i, j: (0, j)),
            pl.BlockSpec((1, tn), lambda b, i, j: (0, 0)),
            pl.BlockSpec((1, tn), lambda b, i, j: (0, 0)),
            pl.BlockSpec((LANES, LANES), lambda b, i, j: (0, 0)),
        ],
        out_specs=pl.BlockSpec((1, tm, tn), lambda b, i, j: (b, i, j)),
        out_shape=jax.ShapeDtypeStruct((bsz, seq, n), BF16),
        scratch_shapes=[pltpu.VMEM((tm, d), BF16)],
        compiler_params=_cparams(("parallel", "parallel", "arbitrary")),
        name="inproj",
    )(x, norm_g.reshape(1, d), sc[:, None, :], sh[:, None, :], w_bf16,
      jnp.tile(q_gain_row, tn // HEAD_DIM).reshape(1, tn),
      jnp.tile(k_gain_row, tn // HEAD_DIM).reshape(1, tn), bd)


def _natten_rows(q_ref, k_refs, v_refs, bias_ref, o_ref, first_row):
    first_head = lax.broadcasted_iota(jnp.int32, (1, LANES), 1) < HEAD_DIM
    zero = jnp.zeros((), BF16)

    def window(refs, row):
        blk, o = divmod(row, ROWS_PER_STEP)
        if o == 0:
            return refs[blk][0]
        return jnp.concatenate([refs[blk][0, o * GRID_W:, :], refs[blk + 1][0, :o * GRID_W, :]], axis=0)

    for rr in range(ROWS_PER_STEP):
        row = first_row[rr]
        start = row - ROWS_PER_STEP - rr + NA_ROWS - 1
        kw = window(k_refs, row)
        vw = window(v_refs, row)
        q = q_ref[0, rr * GRID_W:(rr + 1) * GRID_W, :]
        q2 = jnp.concatenate([jnp.where(first_head, q, zero), jnp.where(first_head, zero, q)], axis=0)
        s = lax.dot_general(q2, kw, (((1,), (1,)), ((), ())), preferred_element_type=F32)
        s = s + jnp.concatenate([bias_ref[0, 0, start], bias_ref[0, 1, start]], axis=0)
        m = jnp.max(s, axis=-1, keepdims=True)
        p = jnp.exp(s - m)
        l = jnp.sum(p, axis=-1, keepdims=True)
        o2 = jnp.dot(p.astype(BF16), vw, preferred_element_type=F32) / l
        o = jnp.where(first_head, o2[:GRID_W], o2[GRID_W:])
        o_ref[0, rr * GRID_W:(rr + 1) * GRID_W, :] = o.astype(o_ref.dtype)


def _natten_kernel(q_ref, k0_ref, k1_ref, k2_ref, v0_ref, v1_ref, v2_ref, bias_ref, o_ref):
    i = pl.program_id(2)
    last = pl.num_programs(2) - 1
    k_refs, v_refs = (k0_ref, k1_ref, k2_ref), (v0_ref, v1_ref, v2_ref)
    half = NA_ROWS // 2
    cases = (
        (i == 0, [ROWS_PER_STEP + max(rr - half, 0) for rr in range(ROWS_PER_STEP)]),
        (i == last, [min(rr + ROWS_PER_STEP - half, ROWS_PER_STEP) for rr in range(ROWS_PER_STEP)]),
        (jnp.logical_and(i > 0, i < last), [rr + ROWS_PER_STEP - half for rr in range(ROWS_PER_STEP)]),
    )
    for cond, first_row in cases:
        @pl.when(cond)
        def _(first_row=first_row):
            _natten_rows(q_ref, k_refs, v_refs, bias_ref, o_ref, first_row)


def _natten_bias(rpb):
    h = rpb.shape[0]
    cols = jnp.arange(GRID_W)
    c0 = jnp.clip(cols - NA_COLS // 2, 0, GRID_W - NA_COLS)
    col_mask = (cols[None, :] >= c0[:, None]) & (cols[None, :] < c0[:, None] + NA_COLS)
    dc = jnp.clip(cols[None, :] - cols[:, None], -(NA_COLS - 1), NA_COLS - 1) + NA_COLS - 1
    rpb_cols = rpb.astype(F32)[:, :, dc]
    dr = jnp.arange(NA_ROWS)[:, None] + jnp.arange(NA_ROWS)[None, :]
    tab = rpb_cols[:, dr]
    tab = jnp.where(col_mask[None, None, None], tab, NEG_INF)
    tab = jnp.transpose(tab, (0, 1, 3, 2, 4)).reshape(h // 2, 2, NA_ROWS, GRID_W, NA_ROWS * GRID_W)
    return tab


def _natten(proj, bias_tab, d_hy, d_att):
    bsz, seq, _ = proj.shape
    rows = seq // GRID_W
    nblk = rows // ROWS_PER_STEP
    assert rows % ROWS_PER_STEP == 0 and rows >= NA_ROWS and nblk >= 2
    hp = d_att // LANES
    qc, kc, vc = (3 * d_hy) // LANES, (3 * d_hy + d_att) // LANES, (3 * d_hy + 2 * d_att) // LANES
    t = TOK_PER_STEP

    def blk(col, shift):
        return pl.BlockSpec(
            (1, t, LANES),
            lambda b, p, i: (b, jnp.clip(i + shift, 0, nblk - 1), col + p))

    return pl.pallas_call(
        _natten_kernel,
        grid=(bsz, hp, nblk),
        in_specs=[blk(qc, 0), blk(kc, -1), blk(kc, 0), blk(kc, 1), blk(vc, -1), blk(vc, 0), blk(vc, 1),
                  pl.BlockSpec((1, 2, NA_ROWS, GRID_W, NA_ROWS * GRID_W), lambda b, p, i: (p, 0, 0, 0, 0))],
        out_specs=pl.BlockSpec((1, t, LANES), lambda b, p, i: (b, i, p)),
        out_shape=jax.ShapeDtypeStruct((bsz, seq, d_att), BF16),
        compiler_params=_cparams(("parallel", "parallel", "arbitrary")),
        name="natten",
    )(proj, proj, proj, proj, proj, proj, proj, bias_tab)


def _outproj_kernel(yh_ref, ya_ref, x_ref, gh_ref, ga_ref, wh_ref, wa_ref, g1_ref, n2_ref, sc_ref, sh_ref,
                    wr_ref, x1_ref, h2_ref, aff_ref):
    def _norm(y_ref, g_ref):
        y = y_ref[0].astype(F32)
        ms = jnp.mean(y * y, axis=-1, keepdims=True)
        return ((y * lax.rsqrt(ms + EPS)) * g_ref[...]).astype(BF16)

    mixed = (jnp.dot(_norm(yh_ref, gh_ref), wh_ref[...], preferred_element_type=F32)
             + jnp.dot(_norm(ya_ref, ga_ref), wa_ref[...], preferred_element_type=F32))
    x1 = x_ref[0] + g1_ref[0] * mixed
    x1_ref[0] = x1
    ms = jnp.mean(x1 * x1, axis=-1, keepdims=True)
    h2 = ((x1 * lax.rsqrt(ms + EPS)) * n2_ref[...]) * (1.0 + sc_ref[0]) + sh_ref[0]
    h2_ref[0] = h2.astype(BF16)
    wr = wr_ref[...]
    w_hi = wr.astype(BF16)
    w_lo = (wr - w_hi.astype(F32)).astype(BF16)
    h_hi = h2.astype(BF16)
    h_lo = (h2 - h_hi.astype(F32)).astype(BF16)
    logits = (jnp.dot(h_hi, w_hi, preferred_element_type=F32)
              + jnp.dot(h_lo, w_hi, preferred_element_type=F32)
              + jnp.dot(h_hi, w_lo, preferred_element_type=F32))
    m = jnp.max(logits, axis=-1, keepdims=True)
    e = jnp.exp(logits - m)
    aff_ref[0] = e / jnp.sum(e, axis=-1, keepdims=True)


def _outproj(y_hy, y_at, x, gain_hy, gain_att, w_out_bf16, g1, norm2_g, sc2, sh2, w_router):
    bsz, seq, d = x.shape
    d_hy, d_att = y_hy.shape[-1], y_at.shape[-1]
    n_exp = w_router.shape[1]
    tm = _largest_tile(seq, 512, 16)
    row = lambda v: v[:, None, :]
    per_b = pl.BlockSpec((1, 1, d), lambda b, i: (b, 0, 0))
    const = lambda shape: pl.BlockSpec(shape, lambda b, i: tuple(0 for _ in shape))
    return pl.pallas_call(
        _outproj_kernel,
        grid=(bsz, seq // tm),
        in_specs=[
            pl.BlockSpec((1, tm, d_hy), lambda b, i: (b, i, 0)),
            pl.BlockSpec((1, tm, d_att), lambda b, i: (b, i, 0)),
            pl.BlockSpec((1, tm, d), lambda b, i: (b, i, 0)),
            const((1, d_hy)), const((1, d_att)),
            const((d_hy, d)), const((d_att, d)),
            per_b, const((1, d)), per_b, per_b,
            const((d, n_exp)),
        ],
        out_specs=[
            pl.BlockSpec((1, tm, d), lambda b, i: (b, i, 0)),
            pl.BlockSpec((1, tm, d), lambda b, i: (b, i, 0)),
            pl.BlockSpec((1, tm, n_exp), lambda b, i: (b, i, 0)),
        ],
        out_shape=[
            jax.ShapeDtypeStruct((bsz, seq, d), F32),
            jax.ShapeDtypeStruct((bsz, seq, d), BF16),
            jax.ShapeDtypeStruct((bsz, seq, n_exp), F32),
        ],
        compiler_params=_cparams(("parallel", "arbitrary")),
        name="outproj",
    )(y_hy, y_at, x, gain_hy.reshape(1, d_hy), gain_att.reshape(1, d_att),
      w_out_bf16[:d_hy], w_out_bf16[d_hy:], row(g1), norm2_g.reshape(1, d), row(sc2), row(sh2), w_router)


FF_TILE = 512
OUT_TILE = 512


def _ffn_kernel(x_ref, wg_ref, wu_ref, wd_ref, gt_ref, o_ref, he_ref, *, ff, nf):
    f = pl.program_id(2)

    @pl.when(f < nf)
    def _():
        x = x_ref[0]
        g = jnp.dot(x, wg_ref[0], preferred_element_type=F32)
        u = jnp.dot(x, wu_ref[0], preferred_element_type=F32)
        he = ((g * jax.nn.sigmoid(g)) * u).astype(BF16)
        he_ref[:, pl.ds(pl.multiple_of(f * FF_TILE, FF_TILE), FF_TILE)] = he

    @pl.when(f >= nf)
    def _():
        y = jnp.dot(he_ref[:, :ff], wd_ref[0], preferred_element_type=F32)
        o_ref[0] = (y * gt_ref[0]).astype(o_ref.dtype)


def _expert_ffn(xg, wg, wu, wd, gates):
    n_exp, cap, d = xg.shape
    ff = wg.shape[2]
    tm = _largest_tile(cap, 1024, 16)
    tn = _largest_tile(d, OUT_TILE, LANES)
    nf = pl.cdiv(ff, FF_TILE)
    up_idx = lambda e, m, f: (e, 0, jnp.minimum(f, nf - 1))
    out_idx = lambda e, m, f: (e, m, jnp.maximum(f - nf, 0))
    return pl.pallas_call(
        functools.partial(_ffn_kernel, ff=ff, nf=nf),
        grid=(n_exp, cap // tm, nf + d // tn),
        in_specs=[
            pl.BlockSpec((1, tm, d), lambda e, m, f: (e, m, 0)),
            pl.BlockSpec((1, d, FF_TILE), up_idx),
            pl.BlockSpec((1, d, FF_TILE), up_idx),
            pl.BlockSpec((1, ff, tn), lambda e, m, f: (e, 0, jnp.maximum(f - nf, 0))),
            pl.BlockSpec((1, tm, 1), lambda e, m, f: (e, m, 0)),
        ],
        out_specs=pl.BlockSpec((1, tm, tn), out_idx),
        out_shape=jax.ShapeDtypeStruct((n_exp, cap, d), BF16),
        scratch_shapes=[pltpu.VMEM((tm, nf * FF_TILE), BF16)],
        compiler_params=_cparams(("parallel", "parallel", "arbitrary")),
        name="expert_ffn",
    )(xg, wg, wu, wd, gates[:, :, None])


def _hyena_filter_taps(seq, w0, b0, w1, b1, w2, b2, w3, freq, d_hy):
    pos = jnp.arange(seq, dtype=F32)
    t = jnp.linspace(0.0, 1.0, seq, dtype=F32)[:, None]
    bands = jnp.linspace(1e-4, HY_BANDS - 1, HY_BANDS, dtype=F32)
    ang = (2.0 * math.pi / seq) * pos[:, None] * bands[None, :]
    z = jnp.concatenate([t, jnp.cos(ang), -jnp.sin(ang)], axis=-1)
    hi = lax.Precision.HIGHEST
    h = jnp.sin(freq * (jnp.dot(z, w0, precision=hi) + b0))
    h = jnp.sin(freq * (jnp.dot(h, w1, precision=hi) + b1))
    h = jnp.sin(freq * (jnp.dot(h, w2, precision=hi) + b2))
    h = jnp.dot(h, w3, precision=hi)
    max_decay = math.log(HY_TARGET) / HY_FAST_DECAY
    min_decay = math.log(HY_TARGET) / HY_SLOW_DECAY
    deltas = jnp.abs(jnp.linspace(min_decay, max_decay, d_hy, dtype=F32))
    decay = jnp.exp(-t * deltas[None, :])
    h_fwd = h[:, :d_hy] * decay
    h_bwd = jnp.where(pos[:, None] > 0, h[:, d_hy:] * decay, 0.0)
    norm = jnp.sum(jnp.abs(h_fwd) + jnp.abs(h_bwd), axis=0, keepdims=True)
    return (jnp.stack([h_fwd, h_bwd], axis=0) / norm).astype(BF16)


DFT_N2 = 128
DFT_COLS = 2048


def _dft_tables(seq):
    n = 2 * seq
    n2 = DFT_N2
    n1 = n // n2
    k1 = jnp.arange(n1, dtype=jnp.int32)
    m1 = jnp.arange(n1 // 2, dtype=jnp.int32)
    ang1 = (2.0 * math.pi / n1) * ((k1[:, None] * m1[None, :]) % n1).astype(F32)
    f1r, f1i = jnp.cos(ang1), -jnp.sin(ang1)
    fwd_a = jnp.concatenate([f1r, f1i], axis=0).astype(BF16)
    inv_a = (jnp.concatenate([f1r.T, f1i.T], axis=1) * (1.0 / n)).astype(BF16)
    k2 = jnp.arange(n2, dtype=jnp.int32)
    ang2 = (2.0 * math.pi / n2) * ((k2[:, None] * k2[None, :]) % n2).astype(F32)
    f2r, f2i = jnp.cos(ang2), -jnp.sin(ang2)
    fwd_c = jnp.block([[f2r, -f2i], [f2i, f2r]]).astype(BF16)
    inv_c = jnp.block([[f2r, f2i], [-f2i, f2r]]).astype(BF16)
    angt = (2.0 * math.pi / n) * (k1[:, None] * k2[None, :]).astype(F32)
    twr, twi = jnp.cos(angt), -jnp.sin(angt)
    return dict(fwd_a=fwd_a, inv_a=inv_a, fwd_c=fwd_c, inv_c=inv_c,
                tw_by_n2=(twr.T[:, :, None], twi.T[:, :, None]),
                tw_by_k1=(twr[:, :, None], twi[:, :, None]))


HALO = 16


def _hy_short_kernel(x1c, x1p, x1n, x2c, x2p, x2n, vc, vp, vn, w_ref, b_ref, vv_ref, x1o_ref):
    i = pl.program_id(1)
    last = pl.num_programs(1) - 1
    tm = x1c.shape[1]
    row = lax.broadcasted_iota(jnp.int32, (tm, 1), 0)

    def conv(cur_ref, prev_ref, next_ref, g):
        cur = cur_ref[0].astype(F32)
        before = jnp.where(i > 0, prev_ref[0, HALO - 1:HALO, :].astype(F32), 0.0)
        after = jnp.where(i < last, next_ref[0, 0:1, :].astype(F32), 0.0)
        up = jnp.where(row == 0, before, pltpu.roll(cur, 1, axis=0))
        dn = jnp.where(row == tm - 1, after, pltpu.roll(cur, tm - 1, axis=0))
        return (w_ref[0, g:g + 1, :] * up + w_ref[1, g:g + 1, :] * cur + w_ref[2, g:g + 1, :] * dn
                + b_ref[g:g + 1, :])

    x1 = conv(x1c, x1p, x1n, 0)
    x2 = conv(x2c, x2p, x2n, 1)
    v = conv(vc, vp, vn, 2)
    vv_ref[0] = (v * x2).astype(vv_ref.dtype)
    x1o_ref[0] = x1.astype(x1o_ref.dtype)


def _hy_short(proj, conv_w, conv_b, d_hy):
    bsz, seq, _ = proj.shape
    tm = _largest_tile(seq, 1024, HALO)
    cb = _largest_tile(d_hy, 512, LANES)
    ncb = d_hy // cb
    nh = seq // HALO

    def specs(g):
        return [
            pl.BlockSpec((1, tm, cb), lambda b, i, c: (b, i, g * ncb + c)),
            pl.BlockSpec((1, HALO, cb), lambda b, i, c: (b, jnp.maximum(i * (tm // HALO) - 1, 0), g * ncb + c)),
            pl.BlockSpec((1, HALO, cb), lambda b, i, c: (b, jnp.minimum((i + 1) * (tm // HALO), nh - 1),
                                                         g * ncb + c)),
        ]

    out_spec = pl.BlockSpec((1, tm, cb), lambda b, i, c: (b, i, c))
    return pl.pallas_call(
        _hy_short_kernel,
        grid=(bsz, seq // tm, ncb),
        in_specs=specs(0) + specs(1) + specs(2) + [
            pl.BlockSpec((3, 3, cb), lambda b, i, c: (0, 0, c)),
            pl.BlockSpec((3, cb), lambda b, i, c: (0, c)),
        ],
        out_specs=[out_spec, out_spec],
        out_shape=[jax.ShapeDtypeStruct((bsz, seq, d_hy), BF16)] * 2,
        compiler_params=_cparams(("parallel", "parallel", "parallel")),
        name="hy_short",
    )(*([proj] * 9), conv_w.reshape(3, 3, d_hy), conv_b.reshape(3, d_hy))


def _hy_fwd_a_kernel(v_ref, fa_ref, twr_ref, twi_ref, o_ref, *, c):
    n1 = fa_ref.shape[0] // 2
    res = jnp.dot(fa_ref[...], v_ref[0], preferred_element_type=F32)
    for s in range(v_ref.shape[2] // c):
        cols = slice(s * c, (s + 1) * c)
        ar, ai = res[:n1, cols], res[n1:, cols]
        tr, ti = twr_ref[s], twi_ref[s]
        o_ref[0, 0, :, cols] = (ar * tr - ai * ti).astype(o_ref.dtype)
        o_ref[0, 1, :, cols] = (ar * ti + ai * tr).astype(o_ref.dtype)


def _hy_fwd_a(vv, tabs):
    bsz, seq, c = vv.shape
    n2 = DFT_N2
    n1 = 2 * seq // n2
    per = max(1, DFT_COLS // c)
    cols = per * c
    twr, twi = tabs["tw_by_n2"]
    return pl.pallas_call(
        functools.partial(_hy_fwd_a_kernel, c=c),
        grid=(bsz, n2 // per),
        in_specs=[
            pl.BlockSpec((1, n1 // 2, cols), lambda b, j: (b, 0, j)),
            pl.BlockSpec((2 * n1, n1 // 2), lambda b, j: (0, 0)),
            pl.BlockSpec((per, n1, 1), lambda b, j: (j, 0, 0)),
            pl.BlockSpec((per, n1, 1), lambda b, j: (j, 0, 0)),
        ],
        out_specs=pl.BlockSpec((1, 2, n1, cols), lambda b, j: (b, 0, 0, j)),
        out_shape=jax.ShapeDtypeStruct((bsz, 2, n1, n2 * c), BF16),
        compiler_params=_cparams(("parallel", "parallel")),
        name="hy_fwd_a",
    )(vv.reshape(bsz, n1 // 2, n2 * c), tabs["fwd_a"], twr, twi)


K1_PER_STEP = 16


def _hy_mid_kernel(a_ref, h_ref, fc_ref, ic_ref, twr_ref, twi_ref, o_ref):
    n2 = DFT_N2
    for k in range(a_ref.shape[2]):
        a = jnp.concatenate([a_ref[0, 0, k], a_ref[0, 1, k]], axis=0)
        x = jnp.dot(fc_ref[...], a, preferred_element_type=F32)
        xr, xi = x[:n2], x[n2:]
        hr, hi = h_ref[0, k].astype(F32), h_ref[1, k].astype(F32)
        y = jnp.concatenate([xr * hr - xi * hi, xr * hi + xi * hr], axis=0).astype(BF16)
        z = jnp.dot(ic_ref[...], y, preferred_element_type=F32)
        zr, zi = z[:n2], z[n2:]
        tr, ti = twr_ref[k], twi_ref[k]
        o_ref[0, 0, k] = (zr * tr + zi * ti).astype(o_ref.dtype)
        o_ref[0, 1, k] = (zi * tr - zr * ti).astype(o_ref.dtype)


def _hy_mid(a, spec_l, tabs):
    bsz, _, n1, n2c = a.shape
    n2 = DFT_N2
    c = n2c // n2
    cb = _largest_tile(c, 256, LANES)
    kc = _largest_tile(n1, K1_PER_STEP, 1)
    twr, twi = tabs["tw_by_k1"]
    return pl.pallas_call(
        _hy_mid_kernel,
        grid=(c // cb, n1 // kc, bsz),
        in_specs=[
            pl.BlockSpec((1, 2, kc, n2, cb), lambda j, k, b: (b, 0, k, 0, j)),
            pl.BlockSpec((2, kc, n2, cb), lambda j, k, b: (0, k, 0, j)),
            pl.BlockSpec((2 * n2, 2 * n2), lambda j, k, b: (0, 0)),
            pl.BlockSpec((2 * n2, 2 * n2), lambda j, k, b: (0, 0)),
            pl.BlockSpec((kc, n2, 1), lambda j, k, b: (k, 0, 0)),
            pl.BlockSpec((kc, n2, 1), lambda j, k, b: (k, 0, 0)),
        ],
        out_specs=pl.BlockSpec((1, 2, kc, n2, cb), lambda j, k, b: (b, 0, k, 0, j)),
        out_shape=jax.ShapeDtypeStruct((bsz, 2, n1, n2, c), BF16),
        compiler_params=_cparams(("parallel", "parallel", "parallel")),
        name="hy_mid",
    )(a.reshape(bsz, 2, n1, n2, c), spec_l, tabs["fwd_c"], tabs["inv_c"], twr, twi)


def _hy_spec_kernel(a_ref, fc_ref, o_ref):
    n2 = DFT_N2
    cb = a_ref.shape[-1]
    for k in range(a_ref.shape[2]):
        fwd = jnp.concatenate([a_ref[0, 0, k], a_ref[0, 1, k]], axis=0)
        bwd = jnp.concatenate([a_ref[1, 0, k], a_ref[1, 1, k]], axis=0)
        x = jnp.dot(fc_ref[...], jnp.concatenate([fwd, bwd], axis=1), preferred_element_type=F32)
        o_ref[0, k] = (x[:n2, :cb] + x[:n2, cb:]).astype(o_ref.dtype)
        o_ref[1, k] = (x[n2:, :cb] - x[n2:, cb:]).astype(o_ref.dtype)


def _hy_spectrum(taps, tabs):
    a = _hy_fwd_a(taps, tabs)
    _, _, n1, n2c = a.shape
    n2 = DFT_N2
    c = n2c // n2
    cb = _largest_tile(c, 256, LANES)
    kc = _largest_tile(n1, K1_PER_STEP, 1)
    return pl.pallas_call(
        _hy_spec_kernel,
        grid=(c // cb, n1 // kc),
        in_specs=[
            pl.BlockSpec((2, 2, kc, n2, cb), lambda j, k: (0, 0, k, 0, j)),
            pl.BlockSpec((2 * n2, 2 * n2), lambda j, k: (0, 0)),
        ],
        out_specs=pl.BlockSpec((2, kc, n2, cb), lambda j, k: (0, k, 0, j)),
        out_shape=jax.ShapeDtypeStruct((2, n1, n2, c), BF16),
        compiler_params=_cparams(("parallel", "parallel")),
        name="hy_spectrum",
    )(a.reshape(2, 2, n1, n2, c), tabs["fwd_c"])


def _hy_inv_a_kernel(z_ref, ga_ref, vv_ref, x1_ref, skip_ref, o_ref):
    y = jnp.dot(ga_ref[...], z_ref[0], preferred_element_type=F32)
    v = vv_ref[0].astype(F32)
    o_ref[0] = ((y + skip_ref[...] * v) * x1_ref[0].astype(F32)).astype(o_ref.dtype)


def _hy_inv_a(z, vv, x1c, skip, tabs):
    bsz, seq, c = vv.shape
    n2 = DFT_N2
    n1 = 2 * seq // n2
    per = max(1, DFT_COLS // c)
    cols = per * c
    flat = lambda t: t.reshape(bsz, n1 // 2, n2 * c)
    data = pl.BlockSpec((1, n1 // 2, cols), lambda b, j: (b, 0, j))
    out = pl.pallas_call(
        _hy_inv_a_kernel,
        grid=(bsz, n2 // per),
        in_specs=[
            pl.BlockSpec((1, 2 * n1, cols), lambda b, j: (b, 0, j)),
            pl.BlockSpec((n1 // 2, 2 * n1), lambda b, j: (0, 0)),
            data, data,
            pl.BlockSpec((1, cols), lambda b, j: (0, 0)),
        ],
        out_specs=data,
        out_shape=jax.ShapeDtypeStruct((bsz, n1 // 2, n2 * c), BF16),
        compiler_params=_cparams(("parallel", "parallel")),
        name="hy_inv_a",
    )(z.reshape(bsz, 2 * n1, n2 * c), tabs["inv_a"], flat(vv), flat(x1c), jnp.tile(skip, per).reshape(1, cols))
    return out.reshape(bsz, seq, c)


def _hyena_mixer(proj, conv_w, conv_b, spec_l, skip, tabs, d_hy):
    vv, x1c = _hy_short(proj, conv_w, conv_b, d_hy)
    a = _hy_fwd_a(vv, tabs)
    z = _hy_mid(a, spec_l, tabs)
    return _hy_inv_a(z, vv, x1c, skip, tabs)


COMBINE_TOKENS = 256
WINDOW = 128
ROW_ALIGN = 16


def _combine_kernel(lo_ref, nwin_ref, slot_ref, x1_ref, g2_ref, ye_hbm, o_ref, buf, acc_ref, sem,
                    *, n_exp, cap):
    i = pl.program_id(0)
    n_tiles = pl.num_programs(0)

    def window_start(tile, e, k):
        return pl.multiple_of(jnp.minimum(lo_ref[tile * n_exp + e] + k * WINDOW, cap - WINDOW), ROW_ALIGN)

    def copy(tile, e, k, slot):
        return pltpu.make_async_copy(ye_hbm.at[e, pl.ds(window_start(tile, e, k), WINDOW), :],
                                     buf.at[slot, pl.ds(e * WINDOW, WINDOW), :], sem.at[slot])

    def fetch(tile, k, slot):
        for e in range(n_exp):
            copy(tile, e, k, slot).start()

    def wait(tile, k, slot):
        for e in range(n_exp):
            copy(tile, e, k, slot).wait()

    def placed(k, slot):
        slots = slot_ref[...]
        j = lax.broadcasted_iota(jnp.int32, (slots.shape[0], WINDOW), 1)
        cols = []
        for e in range(n_exp):
            s = slots[:, e:e + 1]
            lo = lo_ref[i * n_exp + e] + k * WINDOW
            row = jnp.where(s >= lo, s - window_start(i, e, k), -1)
            cols.append(jnp.where(row == j, 1.0, 0.0).astype(BF16))
        return jnp.dot(jnp.concatenate(cols, axis=1), buf[slot], preferred_element_type=F32)

    cur = lax.rem(i, 2)

    @pl.when(i == 0)
    def _():
        fetch(0, 0, 0)

    @pl.when(i + 1 < n_tiles)
    def _():
        fetch(i + 1, 0, 1 - cur)

    wait(i, 0, cur)
    acc_ref[...] = placed(0, cur)

    def extra(k, carry):
        fetch(i, k, 2)
        wait(i, k, 2)
        acc_ref[...] += placed(k, 2)
        return carry

    lax.fori_loop(1, nwin_ref[i], extra, 0)
    o_ref[...] = x1_ref[...] + g2_ref[0] * acc_ref[...]


def _combine(x1, g2, ye, sel):
    bsz, seq, d = x1.shape
    n_exp, cap, _ = ye.shape
    n_tok = bsz * seq
    t = _largest_tile(seq, COMBINE_TOKENS, ROW_ALIGN)
    n_tiles = n_tok // t
    assert cap >= WINDOW and cap % ROW_ALIGN == 0
    slot = jnp.where(sel > 0, jnp.cumsum(sel, axis=1) - 1, -1)
    cnt = sel.reshape(n_exp, n_tiles, t).sum(axis=2)
    first = jnp.cumsum(cnt, axis=1) - cnt
    lo = (first // ROW_ALIGN) * ROW_ALIGN
    nwin = jnp.maximum(jnp.max(-(-(first - lo + cnt) // WINDOW), axis=0), 1)
    tiles_per_seq = seq // t
    grid_spec = pltpu.PrefetchScalarGridSpec(
        num_scalar_prefetch=2,
        grid=(n_tiles,),
        in_specs=[
            pl.BlockSpec((t, n_exp), lambda i, lo_r, nw_r: (i, 0)),
            pl.BlockSpec((t, d), lambda i, lo_r, nw_r: (i, 0)),
            pl.BlockSpec((1, 1, d), lambda i, lo_r, nw_r: (i // tiles_per_seq, 0, 0)),
            pl.BlockSpec(memory_space=pl.ANY),
        ],
        out_specs=pl.BlockSpec((t, d), lambda i, lo_r, nw_r: (i, 0)),
        scratch_shapes=[
            pltpu.VMEM((3, n_exp * WINDOW, d), BF16),
            pltpu.VMEM((t, d), F32),
            pltpu.SemaphoreType.DMA((3,)),
        ],
    )
    out = pl.pallas_call(
        functools.partial(_combine_kernel, n_exp=n_exp, cap=cap),
        grid_spec=grid_spec,
        out_shape=jax.ShapeDtypeStruct((n_tok, d), F32),
        compiler_params=_cparams(("arbitrary",)),
        name="combine",
    )(lo.T.reshape(-1).astype(jnp.int32), nwin.astype(jnp.int32), slot.T.astype(jnp.int32),
      x1.reshape(n_tok, d), g2[:, None, :], ye)
    return out.reshape(bsz, seq, d)


def _layer(x, ada, p):
    bsz, seq, d = x.shape
    d_hy = p["hy_skip"].shape[0]
    d_att = p["out_norm_att"].shape[0]
    n_exp = p["w_router"].shape[1]
    sh1, sc1, g1, sh2, sc2, g2 = jnp.split(ada, N_ADA, axis=-1)

    q_gain = p["q_norm_g"] * (HEAD_DIM ** -0.5)
    proj = _inproj(x, p["norm1_g"], sc1, sh1, p["w_in_bf16"], q_gain, p["k_norm_g"], d_hy, d_att)

    y_hy = _hyena_mixer(proj, p["hy_conv_w"], p["hy_conv_b"], p["spec"], p["hy_skip"], p["dft"], d_hy)
    y_at = _natten(proj, p["bias_tab"], d_hy, d_att)

    x1, h2, aff = _outproj(y_hy, y_at, x, p["out_norm_hy"], p["out_norm_att"], p["w_out_bf16"],
                           g1, p["norm2_g"], sc2, sh2, p["w_router"])

    n_tok = bsz * seq
    cap = CAPACITY_FACTOR * n_tok // n_exp
    aff_t = aff.reshape(n_tok, n_exp).T
    _, idx = lax.top_k(aff_t, cap)
    idx = jnp.sort(idx, axis=1)
    gates = jnp.take_along_axis(aff_t, idx, axis=1)
    sel = jnp.zeros((n_exp, n_tok), jnp.int32).at[jnp.arange(n_exp)[:, None], idx].set(1)
    xg = jnp.take(h2.reshape(n_tok, d), idx, axis=0)
    ye = _expert_ffn(xg, p["wg"], p["wu"], p["wd"], gates)
    return _combine(x1, g2, ye, sel)


def kernel(x_prompt, x_sample, c_prompt, c_sample, ada_w, ada_b, norm1_g, w_in, hy_conv_w, hy_conv_b, hy_f_w0, hy_f_b0, hy_f_w1, hy_f_b1, hy_f_w2, hy_f_b2, hy_f_w3, hy_f_freq, hy_skip, q_norm_g, k_norm_g, rpb, out_norm_hy, out_norm_att, w_out, norm2_g, w_router, w_gate, w_up, w_down):
    depth = ada_w.shape[0]
    y_prompt, y_sample = x_prompt, x_sample
    nbp = x_prompt.shape[0]
    assert x_prompt.shape[1] == x_sample.shape[1]
    seq = x_prompt.shape[1]
    for l in range(depth):
        d_hy = hy_skip.shape[-1]
        dft = _dft_tables(seq)
        taps = _hyena_filter_taps(seq, hy_f_w0[l], hy_f_b0[l], hy_f_w1[l], hy_f_b1[l], hy_f_w2[l],
                                  hy_f_b2[l], hy_f_w3[l], hy_f_freq[l], d_hy)
        p = {
            "norm1_g": norm1_g[l], "w_in_bf16": w_in[l].astype(BF16),
            "hy_conv_w": hy_conv_w[l], "hy_conv_b": hy_conv_b[l], "hy_skip": hy_skip[l],
            "spec": _hy_spectrum(taps, dft), "dft": dft,
            "q_norm_g": q_norm_g[l], "k_norm_g": k_norm_g[l], "bias_tab": _natten_bias(rpb[l]),
            "out_norm_hy": out_norm_hy[l], "out_norm_att": out_norm_att[l],
            "w_out_bf16": w_out[l].astype(BF16), "norm2_g": norm2_g[l], "w_router": w_router[l],
            "wg": w_gate[l].astype(BF16), "wu": w_up[l].astype(BF16), "wd": w_down[l].astype(BF16),
        }
        ada = _ada(jnp.concatenate([c_prompt, c_sample], axis=0), ada_w[l], ada_b[l])
        y_prompt = _layer(y_prompt, ada[:nbp], p)
        y_sample = _layer(y_sample, ada[nbp:], p)
    return (y_prompt, y_sample)
```

```python
import functools
import math

import jax
import jax.numpy as jnp
from jax import lax
from jax.experimental import pallas as pl
from jax.experimental.pallas import tpu as pltpu

F32 = jnp.float32
BF16 = jnp.bfloat16

HEAD_DIM = 64
GRID_W = 64
NA_ROWS = 8
NA_COLS = 16
HY_BANDS = 16
HY_TARGET = 1e-2
HY_FAST_DECAY = 0.3
HY_SLOW_DECAY = 1.5
CAPACITY_FACTOR = 2
N_ADA = 6
EPS = 1e-6
NEG_INF = -1e30

LANES = 128
MXU_DEPTH = 256
VMEM_LIMIT_BYTES = 56 * 1024 * 1024

ROWS_PER_STEP = 8
TOK_PER_STEP = ROWS_PER_STEP * GRID_W


def _largest_tile(n, pref, align):
    if n <= pref:
        return n
    t = (pref // align) * align
    while t > align and n % t:
        t -= align
    assert n % t == 0, (n, pref, align)
    return t


def _cparams(sem):
    return pltpu.CompilerParams(dimension_semantics=sem, vmem_limit_bytes=VMEM_LIMIT_BYTES)


def _ada_kernel(ct_ref, w_ref, b_ref, o_ref):
    ct = ct_ref[...]
    st = ct * jax.nn.sigmoid(ct)
    w = w_ref[...]
    for b in range(ct.shape[1]):
        o_ref[b:b + 1, :] = jnp.sum(w * st[:, b:b + 1], axis=0, keepdims=True) + b_ref[...]


def _ada(c, ada_w, ada_b):
    nb, d = c.shape
    n = ada_w.shape[1]
    tn = _largest_tile(n, 1024, LANES)
    return pl.pallas_call(
        _ada_kernel,
        grid=(n // tn,),
        in_specs=[
            pl.BlockSpec((d, nb), lambda j: (0, 0)),
            pl.BlockSpec((d, tn), lambda j: (0, j)),
            pl.BlockSpec((1, tn), lambda j: (0, j)),
        ],
        out_specs=pl.BlockSpec((nb, tn), lambda j: (0, j)),
        out_shape=jax.ShapeDtypeStruct((nb, n), F32),
        compiler_params=_cparams(("arbitrary",)),
        name="ada",
    )(c.T, ada_w, ada_b.reshape(1, n))


def _inproj_kernel(x_ref, g_ref, sc_ref, sh_ref, w_ref, qg_ref, kg_ref, bd_ref, o_ref, h_scr,
                   *, tn, q_col0, k_col0, v_col0):
    j = pl.program_id(2)

    @pl.when(j == 0)
    def _():
        x = x_ref[0]
        ms = jnp.mean(x * x, axis=-1, keepdims=True)
        y = (x * lax.rsqrt(ms + EPS)) * g_ref[...]
        h_scr[...] = (y * (1.0 + sc_ref[0]) + sh_ref[0]).astype(BF16)

    col0 = j * tn
    is_q = jnp.logical_and(col0 >= q_col0, col0 < k_col0)
    is_k = jnp.logical_and(col0 >= k_col0, col0 < v_col0)
    is_qk = jnp.logical_or(is_q, is_k)

    @pl.when(jnp.logical_not(is_qk))
    def _():
        o_ref[0] = jnp.dot(h_scr[...], w_ref[...], preferred_element_type=F32).astype(o_ref.dtype)

    @pl.when(is_qk)
    def _():
        acc = jnp.dot(h_scr[...], w_ref[...], preferred_element_type=F32)
        gain = jnp.where(is_q, qg_ref[...], kg_ref[...])
        bd = bd_ref[...]
        for c in range(tn // LANES):
            sl = slice(c * LANES, (c + 1) * LANES)
            a = acc[:, sl]
            sq = a * a
            hi = sq.astype(BF16)
            lo = (sq - hi.astype(F32)).astype(BF16)
            ss = (jnp.dot(hi, bd, preferred_element_type=F32)
                  + jnp.dot(lo, bd, preferred_element_type=F32))
            y = (a * lax.rsqrt(ss * (1.0 / HEAD_DIM) + EPS)) * gain[:, sl]
            o_ref[0, :, sl] = y.astype(o_ref.dtype)


def _inproj(x, norm_g, sc, sh, w_bf16, q_gain_row, k_gain_row, d_hy, d_att):
    bsz, seq, d = x.shape
    n = w_bf16.shape[1]
    tm = _largest_tile(seq, 1024, 16)
    tn = _largest_tile(d_att, 1024, LANES)
    assert (3 * d_hy) % tn == 0 and n % tn == 0
    lane = jnp.arange(LANES)
    bd = (lane[:, None] // HEAD_DIM == lane[None, :] // HEAD_DIM).astype(BF16)
    kern = functools.partial(_inproj_kernel, tn=tn, q_col0=3 * d_hy, k_col0=3 * d_hy + d_att,
                             v_col0=3 * d_hy + 2 * d_att)
    return pl.pallas_call(
        kern,
        grid=(bsz, seq // tm, n // tn),
        in_specs=[
            pl.BlockSpec((1, tm, d), lambda b, i, j: (b, i, 0)),
            pl.BlockSpec((1, d), lambda b, i, j: (0, 0)),
            pl.BlockSpec((1, 1, d), lambda b, i, j: (b, 0, 0)),
            pl.BlockSpec((1, 1, d), lambda b, i, j: (b, 0, 0)),
            pl.BlockSpec((d, tn), lambda b, i, j: (0, j)),
            pl.BlockSpec((1, tn), lambda b, i, j: (0, 0)),
            pl.BlockSpec((1, tn), lambda b, i, j: (0, 0)),
            pl.BlockSpec((LANES, LANES), lambda b, i, j: (0, 0)),
        ],
        out_specs=pl.BlockSpec((1, tm, tn), lambda b, i, j: (b, i, j)),
        out_shape=jax.ShapeDtypeStruct((bsz, seq, n), BF16),
        scratch_shapes=[pltpu.VMEM((tm, d), BF16)],
        compiler_params=_cparams(("parallel", "parallel", "arbitrary")),
        name="inproj",
    )(x, norm_g.reshape(1, d), sc[:, None, :], sh[:, None, :], w_bf16,
      jnp.tile(q_gain_row, tn // HEAD_DIM).reshape(1, tn),
      jnp.tile(k_gain_row, tn // HEAD_DIM).reshape(1, tn), bd)


def _natten_rows(q_ref, k_refs, v_refs, bias_ref, o_ref, first_row):
    width = q_ref.shape[-1]
    heads = width // HEAD_DIM
    lane_head = lax.broadcasted_iota(jnp.int32, (1, width), 1) // HEAD_DIM
    zero = jnp.zeros((), BF16)

    def window(refs, row):
        blk, o = divmod(row, ROWS_PER_STEP)
        if o == 0:
            return refs[blk][0]
        return jnp.concatenate([refs[blk][0, o * GRID_W:, :], refs[blk + 1][0, :o * GRID_W, :]], axis=0)

    for rr in range(ROWS_PER_STEP):
        row = first_row[rr]
        start = row - ROWS_PER_STEP - rr + NA_ROWS - 1
        kw = window(k_refs, row)
        vw = window(v_refs, row)
        q = q_ref[0, rr * GRID_W:(rr + 1) * GRID_W, :]
        qs = jnp.concatenate([jnp.where(lane_head == h, q, zero) for h in range(heads)], axis=0)
        s = lax.dot_general(qs, kw, (((1,), (1,)), ((), ())), preferred_element_type=F32)
        s = s + jnp.concatenate([bias_ref[0, h, start] for h in range(heads)], axis=0)
        m = jnp.max(s, axis=-1, keepdims=True)
        p = jnp.exp(s - m)
        l = jnp.sum(p, axis=-1, keepdims=True)
        o_all = jnp.dot(p.astype(BF16), vw, preferred_element_type=F32) / l
        o = o_all[:GRID_W]
        for h in range(1, heads):
            o = jnp.where(lane_head == h, o_all[h * GRID_W:(h + 1) * GRID_W], o)
        o_ref[0, rr * GRID_W:(rr + 1) * GRID_W, :] = o.astype(o_ref.dtype)


def _natten_kernel(q_ref, k0_ref, k1_ref, k2_ref, v0_ref, v1_ref, v2_ref, bias_ref, o_ref):
    i = pl.program_id(2)
    last = pl.num_programs(2) - 1
    k_refs, v_refs = (k0_ref, k1_ref, k2_ref), (v0_ref, v1_ref, v2_ref)
    half = NA_ROWS // 2
    cases = (
        (i == 0, [ROWS_PER_STEP + max(rr - half, 0) for rr in range(ROWS_PER_STEP)]),
        (i == last, [min(rr + ROWS_PER_STEP - half, ROWS_PER_STEP) for rr in range(ROWS_PER_STEP)]),
        (jnp.logical_and(i > 0, i < last), [rr + ROWS_PER_STEP - half for rr in range(ROWS_PER_STEP)]),
    )
    for cond, first_row in cases:
        @pl.when(cond)
        def _(first_row=first_row):
            _natten_rows(q_ref, k_refs, v_refs, bias_ref, o_ref, first_row)


def _natten_heads(n_heads):
    return min(MXU_DEPTH // HEAD_DIM, n_heads)


def _natten_bias(rpb, heads):
    h = rpb.shape[0]
    cols = jnp.arange(GRID_W)
    c0 = jnp.clip(cols - NA_COLS // 2, 0, GRID_W - NA_COLS)
    col_mask = (cols[None, :] >= c0[:, None]) & (cols[None, :] < c0[:, None] + NA_COLS)
    dc = jnp.clip(cols[None, :] - cols[:, None], -(NA_COLS - 1), NA_COLS - 1) + NA_COLS - 1
    rpb_cols = rpb.astype(F32)[:, :, dc]
    dr = jnp.arange(NA_ROWS)[:, None] + jnp.arange(NA_ROWS)[None, :]
    tab = rpb_cols[:, dr]
    tab = jnp.where(col_mask[None, None, None], tab, NEG_INF)
    tab = jnp.transpose(tab, (0, 1, 3, 2, 4))
    return tab.reshape(h // heads, heads, NA_ROWS, GRID_W, NA_ROWS * GRID_W)


def _natten(proj, bias_tab, d_hy, d_att):
    bsz, seq, _ = proj.shape
    rows = seq // GRID_W
    nblk = rows // ROWS_PER_STEP
    assert rows % ROWS_PER_STEP == 0 and rows >= NA_ROWS and nblk >= 2
    heads = bias_tab.shape[1]
    width = heads * HEAD_DIM
    assert d_att % width == 0 and (3 * d_hy) % width == 0
    qc, kc, vc = (3 * d_hy) // width, (3 * d_hy + d_att) // width, (3 * d_hy + 2 * d_att) // width
    t = TOK_PER_STEP

    def blk(col, shift):
        return pl.BlockSpec(
            (1, t, width),
            lambda b, p, i: (b, jnp.clip(i + shift, 0, nblk - 1), col + p))

    return pl.pallas_call(
        _natten_kernel,
        grid=(bsz, d_att // width, nblk),
        in_specs=[blk(qc, 0), blk(kc, -1), blk(kc, 0), blk(kc, 1), blk(vc, -1), blk(vc, 0), blk(vc, 1),
                  pl.BlockSpec((1, heads, NA_ROWS, GRID_W, NA_ROWS * GRID_W), lambda b, p, i: (p, 0, 0, 0, 0))],
        out_specs=pl.BlockSpec((1, t, width), lambda b, p, i: (b, i, p)),
        out_shape=jax.ShapeDtypeStruct((bsz, seq, d_att), BF16),
        compiler_params=_cparams(("parallel", "parallel", "arbitrary")),
        name="natten",
    )(proj, proj, proj, proj, proj, proj, proj, bias_tab)


def _outproj_kernel(yh_ref, ya_ref, x_ref, gh_ref, ga_ref, wh_ref, wa_ref, g1_ref, n2_ref, sc_ref, sh_ref,
                    wr_ref, x1_ref, h2_ref, aff_ref):
    def _norm(y_ref, g_ref):
        y = y_ref[0].astype(F32)
        ms = jnp.mean(y * y, axis=-1, keepdims=True)
        return ((y * lax.rsqrt(ms + EPS)) * g_ref[...]).astype(BF16)

    mixed = (jnp.dot(_norm(yh_ref, gh_ref), wh_ref[...], preferred_element_type=F32)
             + jnp.dot(_norm(ya_ref, ga_ref), wa_ref[...], preferred_element_type=F32))
    x1 = x_ref[0] + g1_ref[0] * mixed
    x1_ref[0] = x1
    ms = jnp.mean(x1 * x1, axis=-1, keepdims=True)
    h2 = ((x1 * lax.rsqrt(ms + EPS)) * n2_ref[...]) * (1.0 + sc_ref[0]) + sh_ref[0]
    h2_ref[0] = h2.astype(BF16)
    wr = wr_ref[...]
    w_hi = wr.astype(BF16)
    w_lo = (wr - w_hi.astype(F32)).astype(BF16)
    h_hi = h2.astype(BF16)
    h_lo = (h2 - h_hi.astype(F32)).astype(BF16)
    n_exp = wr.shape[1]
    both = jnp.dot(h_hi, jnp.concatenate([w_hi, w_lo], axis=1), preferred_element_type=F32)
    logits = both[:, :n_exp] + both[:, n_exp:] + jnp.dot(h_lo, w_hi, preferred_element_type=F32)
    m = jnp.max(logits, axis=-1, keepdims=True)
    e = jnp.exp(logits - m)
    aff_ref[0] = e / jnp.sum(e, axis=-1, keepdims=True)


def _outproj(y_hy, y_at, x, gain_hy, gain_att, w_out_bf16, g1, norm2_g, sc2, sh2, w_router):
    bsz, seq, d = x.shape
    d_hy, d_att = y_hy.shape[-1], y_at.shape[-1]
    n_exp = w_router.shape[1]
    tm = _largest_tile(seq, 512, 16)
    row = lambda v: v[:, None, :]
    per_b = pl.BlockSpec((1, 1, d), lambda b, i: (b, 0, 0))
    const = lambda shape: pl.BlockSpec(shape, lambda b, i: tuple(0 for _ in shape))
    return pl.pallas_call(
        _outproj_kernel,
        grid=(bsz, seq // tm),
        in_specs=[
            pl.BlockSpec((1, tm, d_hy), lambda b, i: (b, i, 0)),
            pl.BlockSpec((1, tm, d_att), lambda b, i: (b, i, 0)),
            pl.BlockSpec((1, tm, d), lambda b, i: (b, i, 0)),
            const((1, d_hy)), const((1, d_att)),
            const((d_hy, d)), const((d_att, d)),
            per_b, const((1, d)), per_b, per_b,
            const((d, n_exp)),
        ],
        out_specs=[
            pl.BlockSpec((1, tm, d), lambda b, i: (b, i, 0)),
            pl.BlockSpec((1, tm, d), lambda b, i: (b, i, 0)),
            pl.BlockSpec((1, tm, n_exp), lambda b, i: (b, i, 0)),
        ],
        out_shape=[
            jax.ShapeDtypeStruct((bsz, seq, d), F32),
            jax.ShapeDtypeStruct((bsz, seq, d), BF16),
            jax.ShapeDtypeStruct((bsz, seq, n_exp), F32),
        ],
        compiler_params=_cparams(("parallel", "arbitrary")),
        name="outproj",
    )(y_hy, y_at, x, gain_hy.reshape(1, d_hy), gain_att.reshape(1, d_att),
      w_out_bf16[:d_hy], w_out_bf16[d_hy:], row(g1), norm2_g.reshape(1, d), row(sc2), row(sh2), w_router)


FF_TILE = 512
OUT_TILE = 512


def _ffn_kernel(x_ref, wg_ref, wu_ref, wd_ref, gt_ref, o_ref, he_ref, *, ff, nf):
    f = pl.program_id(2)

    @pl.when(f < nf)
    def _():
        x = x_ref[0]
        g = jnp.dot(x, wg_ref[0], preferred_element_type=F32)
        u = jnp.dot(x, wu_ref[0], preferred_element_type=F32)
        he = ((g * jax.nn.sigmoid(g)) * u).astype(BF16)
        he_ref[:, pl.ds(pl.multiple_of(f * FF_TILE, FF_TILE), FF_TILE)] = he

    @pl.when(f >= nf)
    def _():
        y = jnp.dot(he_ref[:, :ff], wd_ref[0], preferred_element_type=F32)
        o_ref[0] = (y * gt_ref[0]).astype(o_ref.dtype)


def _expert_ffn(xg, wg, wu, wd, gates):
    n_exp, cap, d = xg.shape
    ff = wg.shape[2]
    tm = _largest_tile(cap, 1024, 16)
    tn = _largest_tile(d, OUT_TILE, LANES)
    nf = pl.cdiv(ff, FF_TILE)
    up_idx = lambda e, m, f: (e, 0, jnp.minimum(f, nf - 1))
    out_idx = lambda e, m, f: (e, m, jnp.maximum(f - nf, 0))
    return pl.pallas_call(
        functools.partial(_ffn_kernel, ff=ff, nf=nf),
        grid=(n_exp, cap // tm, nf + d // tn),
        in_specs=[
            pl.BlockSpec((1, tm, d), lambda e, m, f: (e, m, 0)),
            pl.BlockSpec((1, d, FF_TILE), up_idx),
            pl.BlockSpec((1, d, FF_TILE), up_idx),
            pl.BlockSpec((1, ff, tn), lambda e, m, f: (e, 0, jnp.maximum(f - nf, 0))),
            pl.BlockSpec((1, tm, 1), lambda e, m, f: (e, m, 0)),
        ],
        out_specs=pl.BlockSpec((1, tm, tn), out_idx),
        out_shape=jax.ShapeDtypeStruct((n_exp, cap, d), BF16),
        scratch_shapes=[pltpu.VMEM((tm, nf * FF_TILE), BF16)],
        compiler_params=_cparams(("parallel", "parallel", "arbitrary")),
        name="expert_ffn",
    )(xg, wg, wu, wd, gates[:, :, None])


def _hyena_filter_taps(seq, w0, b0, w1, b1, w2, b2, w3, freq, d_hy):
    pos = jnp.arange(seq, dtype=F32)
    t = jnp.linspace(0.0, 1.0, seq, dtype=F32)[:, None]
    bands = jnp.linspace(1e-4, HY_BANDS - 1, HY_BANDS, dtype=F32)
    ang = (2.0 * math.pi / seq) * pos[:, None] * bands[None, :]
    z = jnp.concatenate([t, jnp.cos(ang), -jnp.sin(ang)], axis=-1)
    hi = lax.Precision.HIGHEST
    h = jnp.sin(freq * (jnp.dot(z, w0, precision=hi) + b0))
    h = jnp.sin(freq * (jnp.dot(h, w1, precision=hi) + b1))
    h = jnp.sin(freq * (jnp.dot(h, w2, precision=hi) + b2))
    h = jnp.dot(h, w3, precision=hi)
    max_decay = math.log(HY_TARGET) / HY_FAST_DECAY
    min_decay = math.log(HY_TARGET) / HY_SLOW_DECAY
    deltas = jnp.abs(jnp.linspace(min_decay, max_decay, d_hy, dtype=F32))
    decay = jnp.exp(-t * deltas[None, :])
    h_fwd = h[:, :d_hy] * decay
    h_bwd = jnp.where(pos[:, None] > 0, h[:, d_hy:] * decay, 0.0)
    norm = jnp.sum(jnp.abs(h_fwd) + jnp.abs(h_bwd), axis=0, keepdims=True)
    return (jnp.stack([h_fwd, h_bwd], axis=0) / norm).astype(BF16)


DFT_N2 = 128
DFT_COLS = 4096
MID_CHANNELS = 256


def _dft_tables(seq):
    n = 2 * seq
    n2 = DFT_N2
    n1 = n // n2
    k1 = jnp.arange(n1, dtype=jnp.int32)
    m1 = jnp.arange(n1 // 2, dtype=jnp.int32)
    ang1 = (2.0 * math.pi / n1) * ((k1[:, None] * m1[None, :]) % n1).astype(F32)
    f1r, f1i = jnp.cos(ang1), -jnp.sin(ang1)
    fwd_a = jnp.concatenate([f1r, f1i], axis=0).astype(BF16)
    inv_a = (jnp.concatenate([f1r.T, f1i.T], axis=1) * (1.0 / n)).astype(BF16)
    k2 = jnp.arange(n2, dtype=jnp.int32)
    ang2 = (2.0 * math.pi / n2) * ((k2[:, None] * k2[None, :]) % n2).astype(F32)
    f2r, f2i = jnp.cos(ang2), -jnp.sin(ang2)
    fwd_c = jnp.block([[f2r, -f2i], [f2i, f2r]]).astype(BF16)
    inv_c = jnp.block([[f2r, f2i], [-f2i, f2r]]).astype(BF16)
    angt = (2.0 * math.pi / n) * (k1[:, None] * k2[None, :]).astype(F32)
    twr, twi = jnp.cos(angt), -jnp.sin(angt)
    return dict(fwd_a=fwd_a, inv_a=inv_a, fwd_c=fwd_c, inv_c=inv_c,
                tw_by_n2=(twr.T[:, :, None], twi.T[:, :, None]),
                tw_by_k1=(twr[:, :, None], twi[:, :, None]))


HALO = 16


def _hy_short_kernel(x1c, x1p, x1n, x2c, x2p, x2n, vc, vp, vn, w_ref, b_ref, vv_ref, x1o_ref):
    i = pl.program_id(1)
    last = pl.num_programs(1) - 1
    tm = x1c.shape[1]
    row = lax.broadcasted_iota(jnp.int32, (tm, 1), 0)

    def conv(cur_ref, prev_ref, next_ref, g):
        cur = cur_ref[0].astype(F32)
        before = jnp.where(i > 0, prev_ref[0, HALO - 1:HALO, :].astype(F32), 0.0)
        after = jnp.where(i < last, next_ref[0, 0:1, :].astype(F32), 0.0)
        up = jnp.where(row == 0, before, pltpu.roll(cur, 1, axis=0))
        dn = jnp.where(row == tm - 1, after, pltpu.roll(cur, tm - 1, axis=0))
        return (w_ref[0, g:g + 1, :] * up + w_ref[1, g:g + 1, :] * cur + w_ref[2, g:g + 1, :] * dn
                + b_ref[g:g + 1, :])

    x1 = conv(x1c, x1p, x1n, 0)
    x2 = conv(x2c, x2p, x2n, 1)
    v = conv(vc, vp, vn, 2)
    vv_ref[0] = (v * x2).astype(vv_ref.dtype)
    x1o_ref[0] = x1.astype(x1o_ref.dtype)


def _hy_short(proj, conv_w, conv_b, d_hy):
    bsz, seq, _ = proj.shape
    tm = _largest_tile(seq, 1024, HALO)
    cb = _largest_tile(d_hy, 512, LANES)
    ncb = d_hy // cb
    nh = seq // HALO

    def specs(g):
        return [
            pl.BlockSpec((1, tm, cb), lambda b, i, c: (b, i, g * ncb + c)),
            pl.BlockSpec((1, HALO, cb), lambda b, i, c: (b, jnp.maximum(i * (tm // HALO) - 1, 0), g * ncb + c)),
            pl.BlockSpec((1, HALO, cb), lambda b, i, c: (b, jnp.minimum((i + 1) * (tm // HALO), nh - 1),
                                                         g * ncb + c)),
        ]

    out_spec = pl.BlockSpec((1, tm, cb), lambda b, i, c: (b, i, c))
    return pl.pallas_call(
        _hy_short_kernel,
        grid=(bsz, seq // tm, ncb),
        in_specs=specs(0) + specs(1) + specs(2) + [
            pl.BlockSpec((3, 3, cb), lambda b, i, c: (0, 0, c)),
            pl.BlockSpec((3, cb), lambda b, i, c: (0, c)),
        ],
        out_specs=[out_spec, out_spec],
        out_shape=[jax.ShapeDtypeStruct((bsz, seq, d_hy), BF16)] * 2,
        compiler_params=_cparams(("parallel", "parallel", "parallel")),
        name="hy_short",
    )(*([proj] * 9), conv_w.reshape(3, 3, d_hy), conv_b.reshape(3, d_hy))


def _hy_fwd_a_kernel(v_ref, fa_ref, twr_ref, twi_ref, o_ref, *, c):
    n1 = fa_ref.shape[0] // 2
    res = jnp.dot(fa_ref[...], v_ref[0], preferred_element_type=F32)
    for s in range(v_ref.shape[2] // c):
        cols = slice(s * c, (s + 1) * c)
        ar, ai = res[:n1, cols], res[n1:, cols]
        tr, ti = twr_ref[s], twi_ref[s]
        o_ref[0, 0, :, cols] = (ar * tr - ai * ti).astype(o_ref.dtype)
        o_ref[0, 1, :, cols] = (ar * ti + ai * tr).astype(o_ref.dtype)


def _hy_fwd_a(vv, tabs):
    bsz, seq, c = vv.shape
    n2 = DFT_N2
    n1 = 2 * seq // n2
    per = max(1, DFT_COLS // c)
    cols = per * c
    twr, twi = tabs["tw_by_n2"]
    return pl.pallas_call(
        functools.partial(_hy_fwd_a_kernel, c=c),
        grid=(bsz, n2 // per),
        in_specs=[
            pl.BlockSpec((1, n1 // 2, cols), lambda b, j: (b, 0, j)),
            pl.BlockSpec((2 * n1, n1 // 2), lambda b, j: (0, 0)),
            pl.BlockSpec((per, n1, 1), lambda b, j: (j, 0, 0)),
            pl.BlockSpec((per, n1, 1), lambda b, j: (j, 0, 0)),
        ],
        out_specs=pl.BlockSpec((1, 2, n1, cols), lambda b, j: (b, 0, 0, j)),
        out_shape=jax.ShapeDtypeStruct((bsz, 2, n1, n2 * c), BF16),
        compiler_params=_cparams(("parallel", "parallel")),
        name="hy_fwd_a",
    )(vv.reshape(bsz, n1 // 2, n2 * c), tabs["fwd_a"], twr, twi)


K1_PER_STEP = 16


def _hy_mid_kernel(a_ref, h_ref, fc_ref, ic_ref, twr_ref, twi_ref, o_ref):
    n2 = DFT_N2
    for k in range(a_ref.shape[2]):
        a = jnp.concatenate([a_ref[0, 0, k], a_ref[0, 1, k]], axis=0)
        x = jnp.dot(fc_ref[...], a, preferred_element_type=F32)
        xr, xi = x[:n2], x[n2:]
        hr, hi = h_ref[0, k].astype(F32), h_ref[1, k].astype(F32)
        y = jnp.concatenate([xr * hr - xi * hi, xr * hi + xi * hr], axis=0).astype(BF16)
        z = jnp.dot(ic_ref[...], y, preferred_element_type=F32)
        zr, zi = z[:n2], z[n2:]
        tr, ti = twr_ref[k], twi_ref[k]
        o_ref[0, 0, k] = (zr * tr + zi * ti).astype(o_ref.dtype)
        o_ref[0, 1, k] = (zi * tr - zr * ti).astype(o_ref.dtype)


def _hy_mid(a, spec_l, tabs):
    bsz, _, n1, n2c = a.shape
    n2 = DFT_N2
    c = n2c // n2
    cb = _largest_tile(c, MID_CHANNELS, LANES)
    kc = _largest_tile(n1, K1_PER_STEP, 1)
    twr, twi = tabs["tw_by_k1"]
    return pl.pallas_call(
        _hy_mid_kernel,
        grid=(c // cb, n1 // kc, bsz),
        in_specs=[
            pl.BlockSpec((1, 2, kc, n2, cb), lambda j, k, b: (b, 0, k, 0, j)),
            pl.BlockSpec((2, kc, n2, cb), lambda j, k, b: (0, k, 0, j)),
            pl.BlockSpec((2 * n2, 2 * n2), lambda j, k, b: (0, 0)),
            pl.BlockSpec((2 * n2, 2 * n2), lambda j, k, b: (0, 0)),
            pl.BlockSpec((kc, n2, 1), lambda j, k, b: (k, 0, 0)),
            pl.BlockSpec((kc, n2, 1), lambda j, k, b: (k, 0, 0)),
        ],
        out_specs=pl.BlockSpec((1, 2, kc, n2, cb), lambda j, k, b: (b, 0, k, 0, j)),
        out_shape=jax.ShapeDtypeStruct((bsz, 2, n1, n2, c), BF16),
        compiler_params=_cparams(("parallel", "parallel", "parallel")),
        name="hy_mid",
    )(a.reshape(bsz, 2, n1, n2, c), spec_l, tabs["fwd_c"], tabs["inv_c"], twr, twi)


def _hy_spec_kernel(a_ref, fc_ref, o_ref):
    n2 = DFT_N2
    cb = a_ref.shape[-1]
    for k in range(a_ref.shape[2]):
        fwd = jnp.concatenate([a_ref[0, 0, k], a_ref[0, 1, k]], axis=0)
        bwd = jnp.concatenate([a_ref[1, 0, k], a_ref[1, 1, k]], axis=0)
        x = jnp.dot(fc_ref[...], jnp.concatenate([fwd, bwd], axis=1), preferred_element_type=F32)
        o_ref[0, k] = (x[:n2, :cb] + x[:n2, cb:]).astype(o_ref.dtype)
        o_ref[1, k] = (x[n2:, :cb] - x[n2:, cb:]).astype(o_ref.dtype)


def _hy_spectrum(taps, tabs):
    a = _hy_fwd_a(taps, tabs)
    _, _, n1, n2c = a.shape
    n2 = DFT_N2
    c = n2c // n2
    cb = _largest_tile(c, MID_CHANNELS, LANES)
    kc = _largest_tile(n1, K1_PER_STEP, 1)
    return pl.pallas_call(
        _hy_spec_kernel,
        grid=(c // cb, n1 // kc),
        in_specs=[
            pl.BlockSpec((2, 2, kc, n2, cb), lambda j, k: (0, 0, k, 0, j)),
            pl.BlockSpec((2 * n2, 2 * n2), lambda j, k: (0, 0)),
        ],
        out_specs=pl.BlockSpec((2, kc, n2, cb), lambda j, k: (0, k, 0, j)),
        out_shape=jax.ShapeDtypeStruct((2, n1, n2, c), BF16),
        compiler_params=_cparams(("parallel", "parallel")),
        name="hy_spectrum",
    )(a.reshape(2, 2, n1, n2, c), tabs["fwd_c"])


def _hy_inv_a_kernel(z_ref, ga_ref, vv_ref, x1_ref, skip_ref, o_ref):
    y = jnp.dot(ga_ref[...], z_ref[0], preferred_element_type=F32)
    v = vv_ref[0].astype(F32)
    o_ref[0] = ((y + skip_ref[...] * v) * x1_ref[0].astype(F32)).astype(o_ref.dtype)


def _hy_inv_a(z, vv, x1c, skip, tabs):
    bsz, seq, c = vv.shape
    n2 = DFT_N2
    n1 = 2 * seq // n2
    per = max(1, DFT_COLS // c)
    cols = per * c
    flat = lambda t: t.reshape(bsz, n1 // 2, n2 * c)
    data = pl.BlockSpec((1, n1 // 2, cols), lambda b, j: (b, 0, j))
    out = pl.pallas_call(
        _hy_inv_a_kernel,
        grid=(bsz, n2 // per),
        in_specs=[
            pl.BlockSpec((1, 2 * n1, cols), lambda b, j: (b, 0, j)),
            pl.BlockSpec((n1 // 2, 2 * n1), lambda b, j: (0, 0)),
            data, data,
            pl.BlockSpec((1, cols), lambda b, j: (0, 0)),
        ],
        out_specs=data,
        out_shape=jax.ShapeDtypeStruct((bsz, n1 // 2, n2 * c), BF16),
        compiler_params=_cparams(("parallel", "parallel")),
        name="hy_inv_a",
    )(z.reshape(bsz, 2 * n1, n2 * c), tabs["inv_a"], flat(vv), flat(x1c), jnp.tile(skip, per).reshape(1, cols))
    return out.reshape(bsz, seq, c)


def _hyena_mixer(proj, conv_w, conv_b, spec_l, skip, tabs, d_hy):
    vv, x1c = _hy_short(proj, conv_w, conv_b, d_hy)
    a = _hy_fwd_a(vv, tabs)
    z = _hy_mid(a, spec_l, tabs)
    return _hy_inv_a(z, vv, x1c, skip, tabs)


COMBINE_TOKENS = 256
WINDOW = 128
ROW_ALIGN = 16


def _combine_kernel(lo_ref, nwin_ref, slot_ref, x1_ref, g2_ref, ye_hbm, o_ref, buf, acc_ref, sem,
                    *, n_exp, cap):
    i = pl.program_id(0)
    n_tiles = pl.num_programs(0)

    def window_start(tile, e, k):
        return pl.multiple_of(jnp.minimum(lo_ref[tile * n_exp + e] + k * WINDOW, cap - WINDOW), ROW_ALIGN)

    def copy(tile, e, k, slot):
        return pltpu.make_async_copy(ye_hbm.at[e, pl.ds(window_start(tile, e, k), WINDOW), :],
                                     buf.at[slot, pl.ds(e * WINDOW, WINDOW), :], sem.at[slot])

    def fetch(tile, k, slot):
        for e in range(n_exp):
            copy(tile, e, k, slot).start()

    def wait(tile, k, slot):
        for e in range(n_exp):
            copy(tile, e, k, slot).wait()

    def placed(k, slot):
        slots = slot_ref[...]
        j = lax.broadcasted_iota(jnp.int32, (slots.shape[0], WINDOW), 1)
        cols = []
        for e in range(n_exp):
            s = slots[:, e:e + 1]
            lo = lo_ref[i * n_exp + e] + k * WINDOW
            row = jnp.where(s >= lo, s - window_start(i, e, k), -1)
            cols.append(jnp.where(row == j, 1.0, 0.0).astype(BF16))
        return jnp.dot(jnp.concatenate(cols, axis=1), buf[slot], preferred_element_type=F32)

    cur = lax.rem(i, 2)

    @pl.when(i == 0)
    def _():
        fetch(0, 0, 0)

    @pl.when(i + 1 < n_tiles)
    def _():
        fetch(i + 1, 0, 1 - cur)

    wait(i, 0, cur)
    acc_ref[...] = placed(0, cur)

    def extra(k, carry):
        fetch(i, k, 2)
        wait(i, k, 2)
        acc_ref[...] += placed(k, 2)
        return carry

    lax.fori_loop(1, nwin_ref[i], extra, 0)
    o_ref[...] = x1_ref[...] + g2_ref[0] * acc_ref[...]


def _combine(x1, g2, ye, sel):
    bsz, seq, d = x1.shape
    n_exp, cap, _ = ye.shape
    n_tok = bsz * seq
    t = _largest_tile(seq, COMBINE_TOKENS, ROW_ALIGN)
    n_tiles = n_tok // t
    assert cap >= WINDOW and cap % ROW_ALIGN == 0
    slot = jnp.where(sel > 0, jnp.cumsum(sel, axis=1) - 1, -1)
    cnt = sel.reshape(n_exp, n_tiles, t).sum(axis=2)
    first = jnp.cumsum(cnt, axis=1) - cnt
    lo = (first // ROW_ALIGN) * ROW_ALIGN
    nwin = jnp.maximum(jnp.max(-(-(first - lo + cnt) // WINDOW), axis=0), 1)
    tiles_per_seq = seq // t
    grid_spec = pltpu.PrefetchScalarGridSpec(
        num_scalar_prefetch=2,
        grid=(n_tiles,),
        in_specs=[
            pl.BlockSpec((t, n_exp), lambda i, lo_r, nw_r: (i, 0)),
            pl.BlockSpec((t, d), lambda i, lo_r, nw_r: (i, 0)),
            pl.BlockSpec((1, 1, d), lambda i, lo_r, nw_r: (i // tiles_per_seq, 0, 0)),
            pl.BlockSpec(memory_space=pl.ANY),
        ],
        out_specs=pl.BlockSpec((t, d), lambda i, lo_r, nw_r: (i, 0)),
        scratch_shapes=[
            pltpu.VMEM((3, n_exp * WINDOW, d), BF16),
            pltpu.VMEM((t, d), F32),
            pltpu.SemaphoreType.DMA((3,)),
        ],
    )
    out = pl.pallas_call(
        functools.partial(_combine_kernel, n_exp=n_exp, cap=cap),
        grid_spec=grid_spec,
        out_shape=jax.ShapeDtypeStruct((n_tok, d), F32),
        compiler_params=_cparams(("arbitrary",)),
        name="combine",
    )(lo.T.reshape(-1).astype(jnp.int32), nwin.astype(jnp.int32), slot.T.astype(jnp.int32),
      x1.reshape(n_tok, d), g2[:, None, :], ye)
    return out.reshape(bsz, seq, d)


def _layer(x, ada, p):
    bsz, seq, d = x.shape
    d_hy = p["hy_skip"].shape[0]
    d_att = p["out_norm_att"].shape[0]
    n_exp = p["w_router"].shape[1]
    sh1, sc1, g1, sh2, sc2, g2 = jnp.split(ada, N_ADA, axis=-1)

    q_gain = p["q_norm_g"] * (HEAD_DIM ** -0.5)
    proj = _inproj(x, p["norm1_g"], sc1, sh1, p["w_in_bf16"], q_gain, p["k_norm_g"], d_hy, d_att)

    y_hy = _hyena_mixer(proj, p["hy_conv_w"], p["hy_conv_b"], p["spec"], p["hy_skip"], p["dft"], d_hy)
    y_at = _natten(proj, p["bias_tab"], d_hy, d_att)

    x1, h2, aff = _outproj(y_hy, y_at, x, p["out_norm_hy"], p["out_norm_att"], p["w_out_bf16"],
                           g1, p["norm2_g"], sc2, sh2, p["w_router"])

    n_tok = bsz * seq
    cap = CAPACITY_FACTOR * n_tok // n_exp
    aff_t = aff.reshape(n_tok, n_exp).T
    _, idx = lax.top_k(aff_t, cap)
    idx = jnp.sort(idx, axis=1)
    gates = jnp.take_along_axis(aff_t, idx, axis=1)
    sel = jnp.zeros((n_exp, n_tok), jnp.int32).at[jnp.arange(n_exp)[:, None], idx].set(1)
    xg = jnp.take(h2.reshape(n_tok, d), idx, axis=0)
    ye = _expert_ffn(xg, p["wg"], p["wu"], p["wd"], gates)
    return _combine(x1, g2, ye, sel)


def kernel(x_prompt, x_sample, c_prompt, c_sample, ada_w, ada_b, norm1_g, w_in, hy_conv_w, hy_conv_b, hy_f_w0, hy_f_b0, hy_f_w1, hy_f_b1, hy_f_w2, hy_f_b2, hy_f_w3, hy_f_freq, hy_skip, q_norm_g, k_norm_g, rpb, out_norm_hy, out_norm_att, w_out, norm2_g, w_router, w_gate, w_up, w_down):
    depth = ada_w.shape[0]
    y_prompt, y_sample = x_prompt, x_sample
    nbp = x_prompt.shape[0]
    assert x_prompt.shape[1] == x_sample.shape[1]
    seq = x_prompt.shape[1]
    for l in range(depth):
        d_hy = hy_skip.shape[-1]
        dft = _dft_tables(seq)
        taps = _hyena_filter_taps(seq, hy_f_w0[l], hy_f_b0[l], hy_f_w1[l], hy_f_b1[l], hy_f_w2[l],
                                  hy_f_b2[l], hy_f_w3[l], hy_f_freq[l], d_hy)
        p = {
            "norm1_g": norm1_g[l], "w_in_bf16": w_in[l].astype(BF16),
            "hy_conv_w": hy_conv_w[l], "hy_conv_b": hy_conv_b[l], "hy_skip": hy_skip[l],
            "spec": _hy_spectrum(taps, dft), "dft": dft,
            "q_norm_g": q_norm_g[l], "k_norm_g": k_norm_g[l], "bias_tab": _natten_bias(rpb[l], _natten_heads(rpb.shape[1])),
            "out_norm_hy": out_norm_hy[l], "out_norm_att": out_norm_att[l],
            "w_out_bf16": w_out[l].astype(BF16), "norm2_g": norm2_g[l], "w_router": w_router[l],
            "wg": w_gate[l].astype(BF16), "wu": w_up[l].astype(BF16), "wd": w_down[l].astype(BF16),
        }
        ada = _ada(jnp.concatenate([c_prompt, c_sample], axis=0), ada_w[l], ada_b[l])
        y_prompt = _layer(y_prompt, ada[:nbp], p)
        y_sample = _layer(y_sample, ada[nbp:], p)
    return (y_prompt, y_sample)
```

```python
import functools
import math

import jax
import jax.numpy as jnp
from jax import lax
from jax.experimental import pallas as pl
from jax.experimental.pallas import tpu as pltpu

F32 = jnp.float32
BF16 = jnp.bfloat16

HEAD_DIM = 64
GRID_W = 64
NA_ROWS = 8
NA_COLS = 16
HY_BANDS = 16
HY_TARGET = 1e-2
HY_FAST_DECAY = 0.3
HY_SLOW_DECAY = 1.5
CAPACITY_FACTOR = 2
N_ADA = 6
EPS = 1e-6
NEG_INF = -1e30

LANES = 128
MXU_DEPTH = 256
BF16_EXACT_INT = 256
VMEM_LIMIT_BYTES = 56 * 1024 * 1024

ROWS_PER_STEP = 8
TOK_PER_STEP = ROWS_PER_STEP * GRID_W


def _largest_tile(n, pref, align):
    if n <= pref:
        return n
    t = (pref // align) * align
    while t > align and n % t:
        t -= align
    assert n % t == 0, (n, pref, align)
    return t


def _cparams(sem):
    return pltpu.CompilerParams(dimension_semantics=sem, vmem_limit_bytes=VMEM_LIMIT_BYTES)


def _ada_kernel(ct_ref, w_ref, b_ref, o_ref):
    ct = ct_ref[...]
    st = ct * jax.nn.sigmoid(ct)
    w = w_ref[...]
    for b in range(ct.shape[1]):
        o_ref[b:b + 1, :] = jnp.sum(w * st[:, b:b + 1], axis=0, keepdims=True) + b_ref[...]


def _ada(c, ada_w, ada_b):
    nb, d = c.shape
    n = ada_w.shape[1]
    tn = _largest_tile(n, 1024, LANES)
    return pl.pallas_call(
        _ada_kernel,
        grid=(n // tn,),
        in_specs=[
            pl.BlockSpec((d, nb), lambda j: (0, 0)),
            pl.BlockSpec((d, tn), lambda j: (0, j)),
            pl.BlockSpec((1, tn), lambda j: (0, j)),
        ],
        out_specs=pl.BlockSpec((nb, tn), lambda j: (0, j)),
        out_shape=jax.ShapeDtypeStruct((nb, n), F32),
        compiler_params=_cparams(("arbitrary",)),
        name="ada",
    )(c.T, ada_w, ada_b.reshape(1, n))


def _inproj_kernel(x_ref, g_ref, sc_ref, sh_ref, w_ref, qg_ref, kg_ref, bd_ref, o_ref, h_scr,
                   *, tn, q_col0, k_col0, v_col0):
    j = pl.program_id(2)

    @pl.when(j == 0)
    def _():
        x = x_ref[0]
        ms = jnp.mean(x * x, axis=-1, keepdims=True)
        y = (x * lax.rsqrt(ms + EPS)) * g_ref[...]
        h_scr[...] = (y * (1.0 + sc_ref[0]) + sh_ref[0]).astype(BF16)

    col0 = j * tn
    is_q = jnp.logical_and(col0 >= q_col0, col0 < k_col0)
    is_k = jnp.logical_and(col0 >= k_col0, col0 < v_col0)
    is_qk = jnp.logical_or(is_q, is_k)

    @pl.when(jnp.logical_not(is_qk))
    def _():
        o_ref[0] = jnp.dot(h_scr[...], w_ref[...], preferred_element_type=F32).astype(o_ref.dtype)

    @pl.when(is_qk)
    def _():
        acc = jnp.dot(h_scr[...], w_ref[...], preferred_element_type=F32)
        gain = jnp.where(is_q, qg_ref[...], kg_ref[...])
        bd = bd_ref[...]
        for c in range(tn // LANES):
            sl = slice(c * LANES, (c + 1) * LANES)
            a = acc[:, sl]
            sq = a * a
            hi = sq.astype(BF16)
            lo = (sq - hi.astype(F32)).astype(BF16)
            ss = (jnp.dot(hi, bd, preferred_element_type=F32)
                  + jnp.dot(lo, bd, preferred_element_type=F32))
            y = (a * lax.rsqrt(ss * (1.0 / HEAD_DIM) + EPS)) * gain[:, sl]
            o_ref[0, :, sl] = y.astype(o_ref.dtype)


def _inproj(x, norm_g, sc, sh, w_bf16, q_gain_row, k_gain_row, d_hy, d_att):
    bsz, seq, d = x.shape
    n = w_bf16.shape[1]
    tm = _largest_tile(seq, 1024, 16)
    tn = _largest_tile(d_att, 1024, LANES)
    assert (3 * d_hy) % tn == 0 and n % tn == 0
    lane = jnp.arange(LANES)
    bd = (lane[:, None] // HEAD_DIM == lane[None, :] // HEAD_DIM).astype(BF16)
    kern = functools.partial(_inproj_kernel, tn=tn, q_col0=3 * d_hy, k_col0=3 * d_hy + d_att,
                             v_col0=3 * d_hy + 2 * d_att)
    return pl.pallas_call(
        kern,
        grid=(bsz, seq // tm, n // tn),
        in_specs=[
            pl.BlockSpec((1, tm, d), lambda b, i, j: (b, i, 0)),
            pl.BlockSpec((1, d), lambda b, i, j: (0, 0)),
            pl.BlockSpec((1, 1, d), lambda b, i, j: (b, 0, 0)),
            pl.BlockSpec((1, 1, d), lambda b, i, j: (b, 0, 0)),
            pl.BlockSpec((d, tn), lambda b, i, j: (0, j)),
            pl.BlockSpec((1, tn), lambda b, i, j: (0, 0)),
            pl.BlockSpec((1, tn), lambda b, i, j: (0, 0)),
            pl.BlockSpec((LANES, LANES), lambda b, i, j: (0, 0)),
        ],
        out_specs=pl.BlockSpec((1, tm, tn), lambda b, i, j: (b, i, j)),
        out_shape=jax.ShapeDtypeStruct((bsz, seq, n), BF16),
        scratch_shapes=[pltpu.VMEM((tm, d), BF16)],
        compiler_params=_cparams(("parallel", "parallel", "arbitrary")),
        name="inproj",
    )(x, norm_g.reshape(1, d), sc[:, None, :], sh[:, None, :], w_bf16,
      jnp.tile(q_gain_row, tn // HEAD_DIM).reshape(1, tn),
      jnp.tile(k_gain_row, tn // HEAD_DIM).reshape(1, tn), bd)


def _natten_rows(q_ref, k_refs, v_refs, bias_ref, o_ref, first_row):
    width = q_ref.shape[-1]
    heads = width // HEAD_DIM
    lane_head = lax.broadcasted_iota(jnp.int32, (1, width), 1) // HEAD_DIM
    zero = jnp.zeros((), BF16)

    def window(refs, row):
        blk, o = divmod(row, ROWS_PER_STEP)
        if o == 0:
            return refs[blk][0]
        return jnp.concatenate([refs[blk][0, o * GRID_W:, :], refs[blk + 1][0, :o * GRID_W, :]], axis=0)

    for rr in range(ROWS_PER_STEP):
        row = first_row[rr]
        start = row - ROWS_PER_STEP - rr + NA_ROWS - 1
        kw = window(k_refs, row)
        vw = window(v_refs, row)
        q = q_ref[0, rr * GRID_W:(rr + 1) * GRID_W, :]
        qs = jnp.concatenate([jnp.where(lane_head == h, q, zero) for h in range(heads)], axis=0)
        s = lax.dot_general(qs, kw, (((1,), (1,)), ((), ())), preferred_element_type=F32)
        s = s + jnp.concatenate([bias_ref[0, h, start] for h in range(heads)], axis=0)
        m = jnp.max(s, axis=-1, keepdims=True)
        p = jnp.exp(s - m)
        l = jnp.sum(p, axis=-1, keepdims=True)
        o_all = jnp.dot(p.astype(BF16), vw, preferred_element_type=F32) / l
        o = o_all[:GRID_W]
        for h in range(1, heads):
            o = jnp.where(lane_head == h, o_all[h * GRID_W:(h + 1) * GRID_W], o)
        o_ref[0, rr * GRID_W:(rr + 1) * GRID_W, :] = o.astype(o_ref.dtype)


def _natten_kernel(q_ref, k0_ref, k1_ref, k2_ref, v0_ref, v1_ref, v2_ref, bias_ref, o_ref):
    i = pl.program_id(2)
    last = pl.num_programs(2) - 1
    k_refs, v_refs = (k0_ref, k1_ref, k2_ref), (v0_ref, v1_ref, v2_ref)
    half = NA_ROWS // 2
    cases = (
        (i == 0, [ROWS_PER_STEP + max(rr - half, 0) for rr in range(ROWS_PER_STEP)]),
        (i == last, [min(rr + ROWS_PER_STEP - half, ROWS_PER_STEP) for rr in range(ROWS_PER_STEP)]),
        (jnp.logical_and(i > 0, i < last), [rr + ROWS_PER_STEP - half for rr in range(ROWS_PER_STEP)]),
    )
    for cond, first_row in cases:
        @pl.when(cond)
        def _(first_row=first_row):
            _natten_rows(q_ref, k_refs, v_refs, bias_ref, o_ref, first_row)


def _natten_heads(n_heads):
    return min(MXU_DEPTH // HEAD_DIM, n_heads)


def _natten_bias(rpb, heads):
    h = rpb.shape[0]
    cols = jnp.arange(GRID_W)
    c0 = jnp.clip(cols - NA_COLS // 2, 0, GRID_W - NA_COLS)
    col_mask = (cols[None, :] >= c0[:, None]) & (cols[None, :] < c0[:, None] + NA_COLS)
    dc = jnp.clip(cols[None, :] - cols[:, None], -(NA_COLS - 1), NA_COLS - 1) + NA_COLS - 1
    rpb_cols = rpb.astype(F32)[:, :, dc]
    dr = jnp.arange(NA_ROWS)[:, None] + jnp.arange(NA_ROWS)[None, :]
    tab = rpb_cols[:, dr]
    tab = jnp.where(col_mask[None, None, None], tab, NEG_INF)
    tab = jnp.transpose(tab, (0, 1, 3, 2, 4))
    return tab.reshape(h // heads, heads, NA_ROWS, GRID_W, NA_ROWS * GRID_W)


def _natten(proj, bias_tab, d_hy, d_att):
    bsz, seq, _ = proj.shape
    rows = seq // GRID_W
    nblk = rows // ROWS_PER_STEP
    assert rows % ROWS_PER_STEP == 0 and rows >= NA_ROWS and nblk >= 2
    heads = bias_tab.shape[1]
    width = heads * HEAD_DIM
    assert d_att % width == 0 and (3 * d_hy) % width == 0
    qc, kc, vc = (3 * d_hy) // width, (3 * d_hy + d_att) // width, (3 * d_hy + 2 * d_att) // width
    t = TOK_PER_STEP

    def blk(col, shift):
        return pl.BlockSpec(
            (1, t, width),
            lambda b, p, i: (b, jnp.clip(i + shift, 0, nblk - 1), col + p))

    return pl.pallas_call(
        _natten_kernel,
        grid=(bsz, d_att // width, nblk),
        in_specs=[blk(qc, 0), blk(kc, -1), blk(kc, 0), blk(kc, 1), blk(vc, -1), blk(vc, 0), blk(vc, 1),
                  pl.BlockSpec((1, heads, NA_ROWS, GRID_W, NA_ROWS * GRID_W), lambda b, p, i: (p, 0, 0, 0, 0))],
        out_specs=pl.BlockSpec((1, t, width), lambda b, p, i: (b, i, p)),
        out_shape=jax.ShapeDtypeStruct((bsz, seq, d_att), BF16),
        compiler_params=_cparams(("parallel", "parallel", "arbitrary")),
        name="natten",
    )(proj, proj, proj, proj, proj, proj, proj, bias_tab)


def _outproj_kernel(yh_ref, ya_ref, x_ref, gh_ref, ga_ref, wh_ref, wa_ref, g1_ref, n2_ref, sc_ref, sh_ref,
                    wr_ref, x1_ref, h2_ref, aff_ref):
    def _norm(y_ref, g_ref):
        y = y_ref[0].astype(F32)
        ms = jnp.mean(y * y, axis=-1, keepdims=True)
        return ((y * lax.rsqrt(ms + EPS)) * g_ref[...]).astype(BF16)

    mixed = (jnp.dot(_norm(yh_ref, gh_ref), wh_ref[...], preferred_element_type=F32)
             + jnp.dot(_norm(ya_ref, ga_ref), wa_ref[...], preferred_element_type=F32))
    x1 = x_ref[0] + g1_ref[0] * mixed
    x1_ref[0] = x1
    ms = jnp.mean(x1 * x1, axis=-1, keepdims=True)
    h2 = ((x1 * lax.rsqrt(ms + EPS)) * n2_ref[...]) * (1.0 + sc_ref[0]) + sh_ref[0]
    h2_ref[0] = h2.astype(BF16)
    wr = wr_ref[...]
    w_hi = wr.astype(BF16)
    w_lo = (wr - w_hi.astype(F32)).astype(BF16)
    h_hi = h2.astype(BF16)
    h_lo = (h2 - h_hi.astype(F32)).astype(BF16)
    n_exp = wr.shape[1]
    both = jnp.dot(h_hi, jnp.concatenate([w_hi, w_lo], axis=1), preferred_element_type=F32)
    logits = both[:, :n_exp] + both[:, n_exp:] + jnp.dot(h_lo, w_hi, preferred_element_type=F32)
    m = jnp.max(logits, axis=-1, keepdims=True)
    e = jnp.exp(logits - m)
    aff_ref[0] = e / jnp.sum(e, axis=-1, keepdims=True)


def _outproj(y_hy, y_at, x, gain_hy, gain_att, w_out_bf16, g1, norm2_g, sc2, sh2, w_router):
    bsz, seq, d = x.shape
    d_hy, d_att = y_hy.shape[-1], y_at.shape[-1]
    n_exp = w_router.shape[1]
    tm = _largest_tile(seq, 512, 16)
    row = lambda v: v[:, None, :]
    per_b = pl.BlockSpec((1, 1, d), lambda b, i: (b, 0, 0))
    const = lambda shape: pl.BlockSpec(shape, lambda b, i: tuple(0 for _ in shape))
    return pl.pallas_call(
        _outproj_kernel,
        grid=(bsz, seq // tm),
        in_specs=[
            pl.BlockSpec((1, tm, d_hy), lambda b, i: (b, i, 0)),
            pl.BlockSpec((1, tm, d_att), lambda b, i: (b, i, 0)),
            pl.BlockSpec((1, tm, d), lambda b, i: (b, i, 0)),
            const((1, d_hy)), const((1, d_att)),
            const((d_hy, d)), const((d_att, d)),
            per_b, const((1, d)), per_b, per_b,
            const((d, n_exp)),
        ],
        out_specs=[
            pl.BlockSpec((1, tm, d), lambda b, i: (b, i, 0)),
            pl.BlockSpec((1, tm, d), lambda b, i: (b, i, 0)),
            pl.BlockSpec((1, tm, n_exp), lambda b, i: (b, i, 0)),
        ],
        out_shape=[
            jax.ShapeDtypeStruct((bsz, seq, d), F32),
            jax.ShapeDtypeStruct((bsz, seq, d), BF16),
            jax.ShapeDtypeStruct((bsz, seq, n_exp), F32),
        ],
        compiler_params=_cparams(("parallel", "arbitrary")),
        name="outproj",
    )(y_hy, y_at, x, gain_hy.reshape(1, d_hy), gain_att.reshape(1, d_att),
      w_out_bf16[:d_hy], w_out_bf16[d_hy:], row(g1), norm2_g.reshape(1, d), row(sc2), row(sh2), w_router)


FF_TILE = 512
OUT_TILE = 512


def _ffn_kernel(x_ref, wg_ref, wu_ref, wd_ref, gt_ref, o_ref, he_ref, *, ff, nf):
    f = pl.program_id(2)

    @pl.when(f < nf)
    def _():
        x = x_ref[0]
        g = jnp.dot(x, wg_ref[0], preferred_element_type=F32)
        u = jnp.dot(x, wu_ref[0], preferred_element_type=F32)
        he = ((g * jax.nn.sigmoid(g)) * u).astype(BF16)
        he_ref[:, pl.ds(pl.multiple_of(f * FF_TILE, FF_TILE), FF_TILE)] = he

    @pl.when(f >= nf)
    def _():
        y = jnp.dot(he_ref[:, :ff], wd_ref[0], preferred_element_type=F32)
        o_ref[0] = (y * gt_ref[0]).astype(o_ref.dtype)


def _expert_ffn(xg, wg, wu, wd, gates):
    n_exp, cap, d = xg.shape
    ff = wg.shape[2]
    tm = _largest_tile(cap, 1024, 16)
    tn = _largest_tile(d, OUT_TILE, LANES)
    nf = pl.cdiv(ff, FF_TILE)
    up_idx = lambda e, m, f: (e, 0, jnp.minimum(f, nf - 1))
    out_idx = lambda e, m, f: (e, m, jnp.maximum(f - nf, 0))
    return pl.pallas_call(
        functools.partial(_ffn_kernel, ff=ff, nf=nf),
        grid=(n_exp, cap // tm, nf + d // tn),
        in_specs=[
            pl.BlockSpec((1, tm, d), lambda e, m, f: (e, m, 0)),
            pl.BlockSpec((1, d, FF_TILE), up_idx),
            pl.BlockSpec((1, d, FF_TILE), up_idx),
            pl.BlockSpec((1, ff, tn), lambda e, m, f: (e, 0, jnp.maximum(f - nf, 0))),
            pl.BlockSpec((1, tm, 1), lambda e, m, f: (e, m, 0)),
        ],
        out_specs=pl.BlockSpec((1, tm, tn), out_idx),
        out_shape=jax.ShapeDtypeStruct((n_exp, cap, d), BF16),
        scratch_shapes=[pltpu.VMEM((tm, nf * FF_TILE), BF16)],
        compiler_params=_cparams(("parallel", "parallel", "arbitrary")),
        name="expert_ffn",
    )(xg, wg, wu, wd, gates[:, :, None])


def _hyena_filter_taps(seq, w0, b0, w1, b1, w2, b2, w3, freq, d_hy):
    pos = jnp.arange(seq, dtype=F32)
    t = jnp.linspace(0.0, 1.0, seq, dtype=F32)[:, None]
    bands = jnp.linspace(1e-4, HY_BANDS - 1, HY_BANDS, dtype=F32)
    ang = (2.0 * math.pi / seq) * pos[:, None] * bands[None, :]
    z = jnp.concatenate([t, jnp.cos(ang), -jnp.sin(ang)], axis=-1)
    hi = lax.Precision.HIGHEST
    h = jnp.sin(freq * (jnp.dot(z, w0, precision=hi) + b0))
    h = jnp.sin(freq * (jnp.dot(h, w1, precision=hi) + b1))
    h = jnp.sin(freq * (jnp.dot(h, w2, precision=hi) + b2))
    h = jnp.dot(h, w3, precision=hi)
    max_decay = math.log(HY_TARGET) / HY_FAST_DECAY
    min_decay = math.log(HY_TARGET) / HY_SLOW_DECAY
    deltas = jnp.abs(jnp.linspace(min_decay, max_decay, d_hy, dtype=F32))
    decay = jnp.exp(-t * deltas[None, :])
    h_fwd = h[:, :d_hy] * decay
    h_bwd = jnp.where(pos[:, None] > 0, h[:, d_hy:] * decay, 0.0)
    norm = jnp.sum(jnp.abs(h_fwd) + jnp.abs(h_bwd), axis=0, keepdims=True)
    return (jnp.stack([h_fwd, h_bwd], axis=0) / norm).astype(BF16)


DFT_N2 = 128
DFT_COLS = 4096
MID_CHANNELS = 256


def _dft_tables(seq):
    n = 2 * seq
    n2 = DFT_N2
    n1 = n // n2
    k1 = jnp.arange(n1, dtype=jnp.int32)
    m1 = jnp.arange(n1 // 2, dtype=jnp.int32)
    ang1 = (2.0 * math.pi / n1) * ((k1[:, None] * m1[None, :]) % n1).astype(F32)
    f1r, f1i = jnp.cos(ang1), -jnp.sin(ang1)
    fwd_a = jnp.concatenate([f1r, f1i], axis=0).astype(BF16)
    inv_a = (jnp.concatenate([f1r.T, f1i.T], axis=1) * (1.0 / n)).astype(BF16)
    k2 = jnp.arange(n2, dtype=jnp.int32)
    ang2 = (2.0 * math.pi / n2) * ((k2[:, None] * k2[None, :]) % n2).astype(F32)
    f2r, f2i = jnp.cos(ang2), -jnp.sin(ang2)
    fwd_c = jnp.block([[f2r, -f2i], [f2i, f2r]]).astype(BF16)
    inv_c = jnp.block([[f2r, f2i], [-f2i, f2r]]).astype(BF16)
    angt = (2.0 * math.pi / n) * (k1[:, None] * k2[None, :]).astype(F32)
    twr, twi = jnp.cos(angt), -jnp.sin(angt)
    return dict(fwd_a=fwd_a, inv_a=inv_a, fwd_c=fwd_c, inv_c=inv_c,
                tw_by_n2=(twr.T[:, :, None], twi.T[:, :, None]),
                tw_by_k1=(twr[:, :, None], twi[:, :, None]))


HALO = 16


def _hy_short_kernel(x1c, x1p, x1n, x2c, x2p, x2n, vc, vp, vn, w_ref, b_ref, vv_ref, x1o_ref):
    i = pl.program_id(1)
    last = pl.num_programs(1) - 1
    tm = x1c.shape[1]
    row = lax.broadcasted_iota(jnp.int32, (tm, 1), 0)

    def conv(cur_ref, prev_ref, next_ref, g):
        cur = cur_ref[0].astype(F32)
        before = jnp.where(i > 0, prev_ref[0, HALO - 1:HALO, :].astype(F32), 0.0)
        after = jnp.where(i < last, next_ref[0, 0:1, :].astype(F32), 0.0)
        up = jnp.where(row == 0, before, pltpu.roll(cur, 1, axis=0))
        dn = jnp.where(row == tm - 1, after, pltpu.roll(cur, tm - 1, axis=0))
        return (w_ref[0, g:g + 1, :] * up + w_ref[1, g:g + 1, :] * cur + w_ref[2, g:g + 1, :] * dn
                + b_ref[g:g + 1, :])

    x1 = conv(x1c, x1p, x1n, 0)
    x2 = conv(x2c, x2p, x2n, 1)
    v = conv(vc, vp, vn, 2)
    vv_ref[0] = (v * x2).astype(vv_ref.dtype)
    x1o_ref[0] = x1.astype(x1o_ref.dtype)


def _hy_short(proj, conv_w, conv_b, d_hy):
    bsz, seq, _ = proj.shape
    tm = _largest_tile(seq, 1024, HALO)
    cb = _largest_tile(d_hy, 512, LANES)
    ncb = d_hy // cb
    nh = seq // HALO

    def specs(g):
        return [
            pl.BlockSpec((1, tm, cb), lambda b, i, c: (b, i, g * ncb + c)),
            pl.BlockSpec((1, HALO, cb), lambda b, i, c: (b, jnp.maximum(i * (tm // HALO) - 1, 0), g * ncb + c)),
            pl.BlockSpec((1, HALO, cb), lambda b, i, c: (b, jnp.minimum((i + 1) * (tm // HALO), nh - 1),
                                                         g * ncb + c)),
        ]

    out_spec = pl.BlockSpec((1, tm, cb), lambda b, i, c: (b, i, c))
    return pl.pallas_call(
        _hy_short_kernel,
        grid=(bsz, seq // tm, ncb),
        in_specs=specs(0) + specs(1) + specs(2) + [
            pl.BlockSpec((3, 3, cb), lambda b, i, c: (0, 0, c)),
            pl.BlockSpec((3, cb), lambda b, i, c: (0, c)),
        ],
        out_specs=[out_spec, out_spec],
        out_shape=[jax.ShapeDtypeStruct((bsz, seq, d_hy), BF16)] * 2,
        compiler_params=_cparams(("parallel", "parallel", "parallel")),
        name="hy_short",
    )(*([proj] * 9), conv_w.reshape(3, 3, d_hy), conv_b.reshape(3, d_hy))


def _hy_fwd_a_kernel(v_ref, fa_ref, twr_ref, twi_ref, o_ref, *, c):
    n1 = fa_ref.shape[0] // 2
    res = jnp.dot(fa_ref[...], v_ref[0], preferred_element_type=F32)
    for s in range(v_ref.shape[2] // c):
        cols = slice(s * c, (s + 1) * c)
        ar, ai = res[:n1, cols], res[n1:, cols]
        tr, ti = twr_ref[s], twi_ref[s]
        o_ref[0, 0, :, cols] = (ar * tr - ai * ti).astype(o_ref.dtype)
        o_ref[0, 1, :, cols] = (ar * ti + ai * tr).astype(o_ref.dtype)


def _hy_fwd_a(vv, tabs):
    bsz, seq, c = vv.shape
    n2 = DFT_N2
    n1 = 2 * seq // n2
    per = max(1, DFT_COLS // c)
    cols = per * c
    twr, twi = tabs["tw_by_n2"]
    return pl.pallas_call(
        functools.partial(_hy_fwd_a_kernel, c=c),
        grid=(bsz, n2 // per),
        in_specs=[
            pl.BlockSpec((1, n1 // 2, cols), lambda b, j: (b, 0, j)),
            pl.BlockSpec((2 * n1, n1 // 2), lambda b, j: (0, 0)),
            pl.BlockSpec((per, n1, 1), lambda b, j: (j, 0, 0)),
            pl.BlockSpec((per, n1, 1), lambda b, j: (j, 0, 0)),
        ],
        out_specs=pl.BlockSpec((1, 2, n1, cols), lambda b, j: (b, 0, 0, j)),
        out_shape=jax.ShapeDtypeStruct((bsz, 2, n1, n2 * c), BF16),
        compiler_params=_cparams(("parallel", "parallel")),
        name="hy_fwd_a",
    )(vv.reshape(bsz, n1 // 2, n2 * c), tabs["fwd_a"], twr, twi)


K1_PER_STEP = 16


def _hy_mid_kernel(a_ref, h_ref, fc_ref, ic_ref, twr_ref, twi_ref, o_ref):
    n2 = DFT_N2
    for k in range(a_ref.shape[2]):
        a = jnp.concatenate([a_ref[0, 0, k], a_ref[0, 1, k]], axis=0)
        x = jnp.dot(fc_ref[...], a, preferred_element_type=F32)
        xr, xi = x[:n2], x[n2:]
        hr, hi = h_ref[0, k].astype(F32), h_ref[1, k].astype(F32)
        y = jnp.concatenate([xr * hr - xi * hi, xr * hi + xi * hr], axis=0).astype(BF16)
        z = jnp.dot(ic_ref[...], y, preferred_element_type=F32)
        zr, zi = z[:n2], z[n2:]
        tr, ti = twr_ref[k], twi_ref[k]
        o_ref[0, 0, k] = (zr * tr + zi * ti).astype(o_ref.dtype)
        o_ref[0, 1, k] = (zi * tr - zr * ti).astype(o_ref.dtype)


def _hy_mid(a, spec_l, tabs):
    bsz, _, n1, n2c = a.shape
    n2 = DFT_N2
    c = n2c // n2
    cb = _largest_tile(c, MID_CHANNELS, LANES)
    kc = _largest_tile(n1, K1_PER_STEP, 1)
    twr, twi = tabs["tw_by_k1"]
    return pl.pallas_call(
        _hy_mid_kernel,
        grid=(c // cb, n1 // kc, bsz),
        in_specs=[
            pl.BlockSpec((1, 2, kc, n2, cb), lambda j, k, b: (b, 0, k, 0, j)),
            pl.BlockSpec((2, kc, n2, cb), lambda j, k, b: (0, k, 0, j)),
            pl.BlockSpec((2 * n2, 2 * n2), lambda j, k, b: (0, 0)),
            pl.BlockSpec((2 * n2, 2 * n2), lambda j, k, b: (0, 0)),
            pl.BlockSpec((kc, n2, 1), lambda j, k, b: (k, 0, 0)),
            pl.BlockSpec((kc, n2, 1), lambda j, k, b: (k, 0, 0)),
        ],
        out_specs=pl.BlockSpec((1, 2, kc, n2, cb), lambda j, k, b: (b, 0, k, 0, j)),
        out_shape=jax.ShapeDtypeStruct((bsz, 2, n1, n2, c), BF16),
        compiler_params=_cparams(("parallel", "parallel", "parallel")),
        name="hy_mid",
    )(a.reshape(bsz, 2, n1, n2, c), spec_l, tabs["fwd_c"], tabs["inv_c"], twr, twi)


def _hy_spec_kernel(a_ref, fc_ref, o_ref):
    n2 = DFT_N2
    cb = a_ref.shape[-1]
    for k in range(a_ref.shape[2]):
        fwd = jnp.concatenate([a_ref[0, 0, k], a_ref[0, 1, k]], axis=0)
        bwd = jnp.concatenate([a_ref[1, 0, k], a_ref[1, 1, k]], axis=0)
        x = jnp.dot(fc_ref[...], jnp.concatenate([fwd, bwd], axis=1), preferred_element_type=F32)
        o_ref[0, k] = (x[:n2, :cb] + x[:n2, cb:]).astype(o_ref.dtype)
        o_ref[1, k] = (x[n2:, :cb] - x[n2:, cb:]).astype(o_ref.dtype)


def _hy_spectrum(taps, tabs):
    a = _hy_fwd_a(taps, tabs)
    _, _, n1, n2c = a.shape
    n2 = DFT_N2
    c = n2c // n2
    cb = _largest_tile(c, MID_CHANNELS, LANES)
    kc = _largest_tile(n1, K1_PER_STEP, 1)
    return pl.pallas_call(
        _hy_spec_kernel,
        grid=(c // cb, n1 // kc),
        in_specs=[
            pl.BlockSpec((2, 2, kc, n2, cb), lambda j, k: (0, 0, k, 0, j)),
            pl.BlockSpec((2 * n2, 2 * n2), lambda j, k: (0, 0)),
        ],
        out_specs=pl.BlockSpec((2, kc, n2, cb), lambda j, k: (0, k, 0, j)),
        out_shape=jax.ShapeDtypeStruct((2, n1, n2, c), BF16),
        compiler_params=_cparams(("parallel", "parallel")),
        name="hy_spectrum",
    )(a.reshape(2, 2, n1, n2, c), tabs["fwd_c"])


def _hy_inv_a_kernel(z_ref, ga_ref, vv_ref, x1_ref, skip_ref, o_ref):
    y = jnp.dot(ga_ref[...], z_ref[0], preferred_element_type=F32)
    v = vv_ref[0].astype(F32)
    o_ref[0] = ((y + skip_ref[...] * v) * x1_ref[0].astype(F32)).astype(o_ref.dtype)


def _hy_inv_a(z, vv, x1c, skip, tabs):
    bsz, seq, c = vv.shape
    n2 = DFT_N2
    n1 = 2 * seq // n2
    per = max(1, DFT_COLS // c)
    cols = per * c
    flat = lambda t: t.reshape(bsz, n1 // 2, n2 * c)
    data = pl.BlockSpec((1, n1 // 2, cols), lambda b, j: (b, 0, j))
    out = pl.pallas_call(
        _hy_inv_a_kernel,
        grid=(bsz, n2 // per),
        in_specs=[
            pl.BlockSpec((1, 2 * n1, cols), lambda b, j: (b, 0, j)),
            pl.BlockSpec((n1 // 2, 2 * n1), lambda b, j: (0, 0)),
            data, data,
            pl.BlockSpec((1, cols), lambda b, j: (0, 0)),
        ],
        out_specs=data,
        out_shape=jax.ShapeDtypeStruct((bsz, n1 // 2, n2 * c), BF16),
        compiler_params=_cparams(("parallel", "parallel")),
        name="hy_inv_a",
    )(z.reshape(bsz, 2 * n1, n2 * c), tabs["inv_a"], flat(vv), flat(x1c), jnp.tile(skip, per).reshape(1, cols))
    return out.reshape(bsz, seq, c)


def _hyena_mixer(proj, conv_w, conv_b, spec_l, skip, tabs, d_hy):
    vv, x1c = _hy_short(proj, conv_w, conv_b, d_hy)
    a = _hy_fwd_a(vv, tabs)
    z = _hy_mid(a, spec_l, tabs)
    return _hy_inv_a(z, vv, x1c, skip, tabs)


def _route_kernel(a_ref, upper_ref, lower_ref, sel_ref, idx_ref, slot_scr, *, cap):
    a = a_ref[0]
    n_rows = a.shape[0]
    bits = pltpu.bitcast(a, jnp.int32)
    thr = jnp.int32(0)
    for b in range(30, -1, -1):
        cand = thr | jnp.int32(1 << b)
        cnt = jnp.sum(jnp.where(bits >= cand, 1.0, 0.0))
        thr = jnp.where(cnt >= cap, cand, thr)
    gt = jnp.where(bits > thr, 1.0, 0.0)
    eq = jnp.where(bits == thr, 1.0, 0.0)
    need = cap - jnp.sum(gt)

    def prefix(m):
        within = jnp.dot(m.astype(BF16), upper_ref[...], preferred_element_type=F32)
        totals = jnp.broadcast_to(within[:, LANES - 1:LANES], within.shape).astype(BF16)
        return within + jnp.dot(lower_ref[...], totals, preferred_element_type=F32)

    sel = gt + eq * jnp.where(prefix(eq) <= need, 1.0, 0.0)
    slot = jnp.where(sel > 0.0, prefix(sel) - 1.0, -1.0)
    sel_ref[0] = sel.astype(jnp.int32)
    slot_scr[...] = slot

    out_rows = idx_ref.shape[1]
    row_id = lax.broadcasted_iota(jnp.int32, (out_rows, 2 * LANES), 0).astype(F32)
    col_id = lax.broadcasted_iota(jnp.int32, (LANES, 2 * LANES), 0).astype(F32)
    lane = lax.broadcasted_iota(jnp.int32, (1, LANES), 1).astype(F32)
    lanes2 = jnp.concatenate([lane, lane], axis=1)

    def body(g, acc):
        pair = slot_scr[pl.ds(2 * g, 2), :]
        s = jnp.concatenate([pair[0:1], pair[1:2]], axis=1)
        hi = jnp.floor(s * (1.0 / LANES))
        lo = s - hi * LANES
        base = jnp.asarray(2 * g).astype(F32)
        rows = jnp.concatenate([jnp.full((1, LANES), 0.0, F32), jnp.full((1, LANES), 1.0, F32)], axis=1) + base
        in_row = hi == row_id
        lhs = jnp.concatenate([jnp.where(in_row, rows, 0.0), jnp.where(in_row, lanes2, 0.0)], axis=0)
        rhs = jnp.where(lo == col_id, 1.0, 0.0)
        return acc + lax.dot_general(lhs.astype(BF16), rhs.astype(BF16), (((1,), (1,)), ((), ())),
                                     preferred_element_type=F32)

    acc = lax.fori_loop(0, n_rows // 2, body, jnp.zeros((2 * out_rows, LANES), F32), unroll=4)
    idx_ref[0] = (acc[:out_rows] * LANES + acc[out_rows:]).astype(jnp.int32)


def _route(aff_t, cap):
    n_exp, n_tok = aff_t.shape
    n_rows = n_tok // LANES
    assert n_tok % (2 * LANES) == 0 and cap % LANES == 0
    assert n_rows <= BF16_EXACT_INT
    i = jnp.arange(LANES)
    upper = (i[:, None] <= i[None, :]).astype(BF16)
    r = jnp.arange(n_rows)
    lower = (r[:, None] > r[None, :]).astype(BF16)
    sel, idx = pl.pallas_call(
        functools.partial(_route_kernel, cap=cap),
        grid=(n_exp,),
        in_specs=[
            pl.BlockSpec((1, n_rows, LANES), lambda e: (e, 0, 0)),
            pl.BlockSpec((LANES, LANES), lambda e: (0, 0)),
            pl.BlockSpec((n_rows, n_rows), lambda e: (0, 0)),
        ],
        out_specs=[
            pl.BlockSpec((1, n_rows, LANES), lambda e: (e, 0, 0)),
            pl.BlockSpec((1, cap // LANES, LANES), lambda e: (e, 0, 0)),
        ],
        out_shape=[
            jax.ShapeDtypeStruct((n_exp, n_rows, LANES), jnp.int32),
            jax.ShapeDtypeStruct((n_exp, cap // LANES, LANES), jnp.int32),
        ],
        scratch_shapes=[pltpu.VMEM((n_rows, LANES), F32)],
        compiler_params=_cparams(("parallel",)),
        name="route",
    )(aff_t.reshape(n_exp, n_rows, LANES), upper, lower)
    return sel.reshape(n_exp, n_tok), idx.reshape(n_exp, cap)


COMBINE_TOKENS = 512
WINDOW = 128
ROW_ALIGN = 16


def _combine_kernel(lo_ref, nwin_ref, slot_ref, x1_ref, g2_ref, ye_hbm, o_ref, buf, acc_ref, sem,
                    *, n_exp, cap):
    i = pl.program_id(0)
    n_tiles = pl.num_programs(0)

    def window_start(tile, e, k):
        return pl.multiple_of(jnp.minimum(lo_ref[tile * n_exp + e] + k * WINDOW, cap - WINDOW), ROW_ALIGN)

    def copy(tile, e, k, slot):
        return pltpu.make_async_copy(ye_hbm.at[e, pl.ds(window_start(tile, e, k), WINDOW), :],
                                     buf.at[slot, pl.ds(e * WINDOW, WINDOW), :], sem.at[slot])

    def fetch(tile, k, slot):
        for e in range(n_exp):
            copy(tile, e, k, slot).start()

    def wait(tile, k, slot):
        for e in range(n_exp):
            copy(tile, e, k, slot).wait()

    def placed(k, slot):
        slots = slot_ref[...]
        j = lax.broadcasted_iota(jnp.int32, (slots.shape[0], WINDOW), 1)
        cols = []
        for e in range(n_exp):
            s = slots[:, e:e + 1]
            lo = lo_ref[i * n_exp + e] + k * WINDOW
            row = jnp.where(s >= lo, s - window_start(i, e, k), -1)
            cols.append(jnp.where(row == j, 1.0, 0.0).astype(BF16))
        return jnp.dot(jnp.concatenate(cols, axis=1), buf[slot], preferred_element_type=F32)

    cur = lax.rem(i, 2)

    @pl.when(i == 0)
    def _():
        fetch(0, 0, 0)

    @pl.when(i + 1 < n_tiles)
    def _():
        fetch(i + 1, 0, 1 - cur)

    wait(i, 0, cur)
    acc_ref[...] = placed(0, cur)

    def extra(k, carry):
        fetch(i, k, 2)
        wait(i, k, 2)
        acc_ref[...] += placed(k, 2)
        return carry

    lax.fori_loop(1, nwin_ref[i], extra, 0)
    o_ref[...] = x1_ref[...] + g2_ref[0] * acc_ref[...]


def _combine(x1, g2, ye, sel):
    bsz, seq, d = x1.shape
    n_exp, cap, _ = ye.shape
    n_tok = bsz * seq
    t = _largest_tile(seq, COMBINE_TOKENS, ROW_ALIGN)
    n_tiles = n_tok // t
    assert cap >= WINDOW and cap % ROW_ALIGN == 0
    slot = jnp.where(sel > 0, jnp.cumsum(sel, axis=1) - 1, -1)
    cnt = sel.reshape(n_exp, n_tiles, t).sum(axis=2)
    first = jnp.cumsum(cnt, axis=1) - cnt
    lo = (first // ROW_ALIGN) * ROW_ALIGN
    nwin = jnp.maximum(jnp.max(-(-(first - lo + cnt) // WINDOW), axis=0), 1)
    tiles_per_seq = seq // t
    grid_spec = pltpu.PrefetchScalarGridSpec(
        num_scalar_prefetch=2,
        grid=(n_tiles,),
        in_specs=[
            pl.BlockSpec((t, n_exp), lambda i, lo_r, nw_r: (i, 0)),
            pl.BlockSpec((t, d), lambda i, lo_r, nw_r: (i, 0)),
            pl.BlockSpec((1, 1, d), lambda i, lo_r, nw_r: (i // tiles_per_seq, 0, 0)),
            pl.BlockSpec(memory_space=pl.ANY),
        ],
        out_specs=pl.BlockSpec((t, d), lambda i, lo_r, nw_r: (i, 0)),
        scratch_shapes=[
            pltpu.VMEM((3, n_exp * WINDOW, d), BF16),
            pltpu.VMEM((t, d), F32),
            pltpu.SemaphoreType.DMA((3,)),
        ],
    )
    out = pl.pallas_call(
        functools.partial(_combine_kernel, n_exp=n_exp, cap=cap),
        grid_spec=grid_spec,
        out_shape=jax.ShapeDtypeStruct((n_tok, d), F32),
        compiler_params=_cparams(("arbitrary",)),
        name="combine",
    )(lo.T.reshape(-1).astype(jnp.int32), nwin.astype(jnp.int32), slot.T.astype(jnp.int32),
      x1.reshape(n_tok, d), g2[:, None, :], ye)
    return out.reshape(bsz, seq, d)


def _layer(x, ada, p):
    bsz, seq, d = x.shape
    d_hy = p["hy_skip"].shape[0]
    d_att = p["out_norm_att"].shape[0]
    n_exp = p["w_router"].shape[1]
    sh1, sc1, g1, sh2, sc2, g2 = jnp.split(ada, N_ADA, axis=-1)

    q_gain = p["q_norm_g"] * (HEAD_DIM ** -0.5)
    proj = _inproj(x, p["norm1_g"], sc1, sh1, p["w_in_bf16"], q_gain, p["k_norm_g"], d_hy, d_att)

    y_hy = _hyena_mixer(proj, p["hy_conv_w"], p["hy_conv_b"], p["spec"], p["hy_skip"], p["dft"], d_hy)
    y_at = _natten(proj, p["bias_tab"], d_hy, d_att)

    x1, h2, aff = _outproj(y_hy, y_at, x, p["out_norm_hy"], p["out_norm_att"], p["w_out_bf16"],
                           g1, p["norm2_g"], sc2, sh2, p["w_router"])

    n_tok = bsz * seq
    cap = CAPACITY_FACTOR * n_tok // n_exp
    aff_t = aff.reshape(n_tok, n_exp).T
    sel, idx = _route(aff_t, cap)
    gates = jnp.take_along_axis(aff_t, idx, axis=1)
    xg = jnp.take(h2.reshape(n_tok, d), idx, axis=0)
    ye = _expert_ffn(xg, p["wg"], p["wu"], p["wd"], gates)
    return _combine(x1, g2, ye, sel)


def kernel(x_prompt, x_sample, c_prompt, c_sample, ada_w, ada_b, norm1_g, w_in, hy_conv_w, hy_conv_b, hy_f_w0, hy_f_b0, hy_f_w1, hy_f_b1, hy_f_w2, hy_f_b2, hy_f_w3, hy_f_freq, hy_skip, q_norm_g, k_norm_g, rpb, out_norm_hy, out_norm_att, w_out, norm2_g, w_router, w_gate, w_up, w_down):
    depth = ada_w.shape[0]
    y_prompt, y_sample = x_prompt, x_sample
    nbp = x_prompt.shape[0]
    assert x_prompt.shape[1] == x_sample.shape[1]
    seq = x_prompt.shape[1]
    for l in range(depth):
        d_hy = hy_skip.shape[-1]
        dft = _dft_tables(seq)
        taps = _hyena_filter_taps(seq, hy_f_w0[l], hy_f_b0[l], hy_f_w1[l], hy_f_b1[l], hy_f_w2[l],
                                  hy_f_b2[l], hy_f_w3[l], hy_f_freq[l], d_hy)
        p = {
            "norm1_g": norm1_g[l], "w_in_bf16": w_in[l].astype(BF16),
            "hy_conv_w": hy_conv_w[l], "hy_conv_b": hy_conv_b[l], "hy_skip": hy_skip[l],
            "spec": _hy_spectrum(taps, dft), "dft": dft,
            "q_norm_g": q_norm_g[l], "k_norm_g": k_norm_g[l], "bias_tab": _natten_bias(rpb[l], _natten_heads(rpb.shape[1])),
            "out_norm_hy": out_norm_hy[l], "out_norm_att": out_norm_att[l],
            "w_out_bf16": w_out[l].astype(BF16), "norm2_g": norm2_g[l], "w_router": w_router[l],
            "wg": w_gate[l].astype(BF16), "wu": w_up[l].astype(BF16), "wd": w_down[l].astype(BF16),
        }
        ada = _ada(jnp.concatenate([c_prompt, c_sample], axis=0), ada_w[l], ada_b[l])
        y_prompt = _layer(y_prompt, ada[:nbp], p)
        y_sample = _layer(y_sample, ada[nbp:], p)
    return (y_prompt, y_sample)
```

```python
import functools
import math

import jax
import jax.numpy as jnp
from jax import lax
from jax.experimental import pallas as pl
from jax.experimental.pallas import tpu as pltpu

F32 = jnp.float32
BF16 = jnp.bfloat16

HEAD_DIM = 64
GRID_W = 64
NA_ROWS = 8
NA_COLS = 16
HY_BANDS = 16
HY_TARGET = 1e-2
HY_FAST_DECAY = 0.3
HY_SLOW_DECAY = 1.5
CAPACITY_FACTOR = 2
N_ADA = 6
EPS = 1e-6
NEG_INF = -1e30

LANES = 128
MXU_DEPTH = 256
BF16_EXACT_INT = 256
VMEM_LIMIT_BYTES = 56 * 1024 * 1024

ROWS_PER_STEP = 8
TOK_PER_STEP = ROWS_PER_STEP * GRID_W


def _largest_tile(n, pref, align):
    if n <= pref:
        return n
    t = (pref // align) * align
    while t > align and n % t:
        t -= align
    assert n % t == 0, (n, pref, align)
    return t


def _cparams(sem):
    return pltpu.CompilerParams(dimension_semantics=sem, vmem_limit_bytes=VMEM_LIMIT_BYTES)


def _ada_kernel(ct_ref, w_ref, b_ref, o_ref):
    ct = ct_ref[...]
    st = ct * jax.nn.sigmoid(ct)
    w = w_ref[...]
    for b in range(ct.shape[1]):
        o_ref[b:b + 1, :] = jnp.sum(w * st[:, b:b + 1], axis=0, keepdims=True) + b_ref[...]


def _ada(c, ada_w, ada_b):
    nb, d = c.shape
    n = ada_w.shape[1]
    tn = _largest_tile(n, 1024, LANES)
    return pl.pallas_call(
        _ada_kernel,
        grid=(n // tn,),
        in_specs=[
            pl.BlockSpec((d, nb), lambda j: (0, 0)),
            pl.BlockSpec((d, tn), lambda j: (0, j)),
            pl.BlockSpec((1, tn), lambda j: (0, j)),
        ],
        out_specs=pl.BlockSpec((nb, tn), lambda j: (0, j)),
        out_shape=jax.ShapeDtypeStruct((nb, n), F32),
        compiler_params=_cparams(("arbitrary",)),
        name="ada",
    )(c.T, ada_w, ada_b.reshape(1, n))


def _inproj_kernel(x_ref, g_ref, sc_ref, sh_ref, w_ref, qg_ref, kg_ref, bd_ref, o_ref, h_scr,
                   *, tn, q_col0, k_col0, v_col0):
    j = pl.program_id(2)

    @pl.when(j == 0)
    def _():
        x = x_ref[0]
        ms = jnp.mean(x * x, axis=-1, keepdims=True)
        y = (x * lax.rsqrt(ms + EPS)) * g_ref[...]
        h_scr[...] = (y * (1.0 + sc_ref[0]) + sh_ref[0]).astype(BF16)

    col0 = j * tn
    is_q = jnp.logical_and(col0 >= q_col0, col0 < k_col0)
    is_k = jnp.logical_and(col0 >= k_col0, col0 < v_col0)
    is_qk = jnp.logical_or(is_q, is_k)

    @pl.when(jnp.logical_not(is_qk))
    def _():
        o_ref[0] = jnp.dot(h_scr[...], w_ref[...], preferred_element_type=F32).astype(o_ref.dtype)

    @pl.when(is_qk)
    def _():
        acc = jnp.dot(h_scr[...], w_ref[...], preferred_element_type=F32)
        gain = jnp.where(is_q, qg_ref[...], kg_ref[...])
        bd = bd_ref[...]
        for c in range(tn // LANES):
            sl = slice(c * LANES, (c + 1) * LANES)
            a = acc[:, sl]
            sq = a * a
            hi = sq.astype(BF16)
            lo = (sq - hi.astype(F32)).astype(BF16)
            ss = (jnp.dot(hi, bd, preferred_element_type=F32)
                  + jnp.dot(lo, bd, preferred_element_type=F32))
            y = (a * lax.rsqrt(ss * (1.0 / HEAD_DIM) + EPS)) * gain[:, sl]
            o_ref[0, :, sl] = y.astype(o_ref.dtype)


def _inproj(x, norm_g, sc, sh, w_bf16, q_gain_row, k_gain_row, d_hy, d_att):
    bsz, seq, d = x.shape
    n = w_bf16.shape[1]
    tm = _largest_tile(seq, 1024, 16)
    tn = _largest_tile(d_att, 1024, LANES)
    assert (3 * d_hy) % tn == 0 and n % tn == 0
    lane = jnp.arange(LANES)
    bd = (lane[:, None] // HEAD_DIM == lane[None, :] // HEAD_DIM).astype(BF16)
    kern = functools.partial(_inproj_kernel, tn=tn, q_col0=3 * d_hy, k_col0=3 * d_hy + d_att,
                             v_col0=3 * d_hy + 2 * d_att)
    return pl.pallas_call(
        kern,
        grid=(bsz, seq // tm, n // tn),
        in_specs=[
            pl.BlockSpec((1, tm, d), lambda b, i, j: (b, i, 0)),
            pl.BlockSpec((1, d), lambda b, i, j: (0, 0)),
            pl.BlockSpec((1, 1, d), lambda b, i, j: (b, 0, 0)),
            pl.BlockSpec((1, 1, d), lambda b, i, j: (b, 0, 0)),
            pl.BlockSpec((d, tn), lambda b, i, j: (0, j)),
            pl.BlockSpec((1, tn), lambda b, i, j: (0, 0)),
            pl.BlockSpec((1, tn), lambda b, i, j: (0, 0)),
            pl.BlockSpec((LANES, LANES), lambda b, i, j: (0, 0)),
        ],
        out_specs=pl.BlockSpec((1, tm, tn), lambda b, i, j: (b, i, j)),
        out_shape=jax.ShapeDtypeStruct((bsz, seq, n), BF16),
        scratch_shapes=[pltpu.VMEM((tm, d), BF16)],
        compiler_params=_cparams(("parallel", "parallel", "arbitrary")),
        name="inproj",
    )(x, norm_g.reshape(1, d), sc[:, None, :], sh[:, None, :], w_bf16,
      jnp.tile(q_gain_row, tn // HEAD_DIM).reshape(1, tn),
      jnp.tile(k_gain_row, tn // HEAD_DIM).reshape(1, tn), bd)


def _natten_rows(q_ref, k_refs, v_refs, bias_ref, o_ref, first_row):
    width = q_ref.shape[-1]
    heads = width // HEAD_DIM
    lane_head = lax.broadcasted_iota(jnp.int32, (1, width), 1) // HEAD_DIM
    zero = jnp.zeros((), BF16)

    def window(refs, row):
        blk, o = divmod(row, ROWS_PER_STEP)
        if o == 0:
            return refs[blk][0]
        return jnp.concatenate([refs[blk][0, o * GRID_W:, :], refs[blk + 1][0, :o * GRID_W, :]], axis=0)

    for rr in range(ROWS_PER_STEP):
        row = first_row[rr]
        start = row - ROWS_PER_STEP - rr + NA_ROWS - 1
        kw = window(k_refs, row)
        vw = window(v_refs, row)
        q = q_ref[0, rr * GRID_W:(rr + 1) * GRID_W, :]
        qs = jnp.concatenate([jnp.where(lane_head == h, q, zero) for h in range(heads)], axis=0)
        s = lax.dot_general(qs, kw, (((1,), (1,)), ((), ())), preferred_element_type=F32)
        s = s + jnp.concatenate([bias_ref[0, h, start] for h in range(heads)], axis=0)
        m = jnp.max(s, axis=-1, keepdims=True)
        p = jnp.exp(s - m)
        l = jnp.sum(p, axis=-1, keepdims=True)
        o_all = jnp.dot(p.astype(BF16), vw, preferred_element_type=F32) / l
        o = o_all[:GRID_W]
        for h in range(1, heads):
            o = jnp.where(lane_head == h, o_all[h * GRID_W:(h + 1) * GRID_W], o)
        o_ref[0, rr * GRID_W:(rr + 1) * GRID_W, :] = o.astype(o_ref.dtype)


def _natten_kernel(q_ref, k0_ref, k1_ref, k2_ref, v0_ref, v1_ref, v2_ref, bias_ref, o_ref):
    i = pl.program_id(2)
    last = pl.num_programs(2) - 1
    k_refs, v_refs = (k0_ref, k1_ref, k2_ref), (v0_ref, v1_ref, v2_ref)
    half = NA_ROWS // 2
    cases = (
        (i == 0, [ROWS_PER_STEP + max(rr - half, 0) for rr in range(ROWS_PER_STEP)]),
        (i == last, [min(rr + ROWS_PER_STEP - half, ROWS_PER_STEP) for rr in range(ROWS_PER_STEP)]),
        (jnp.logical_and(i > 0, i < last), [rr + ROWS_PER_STEP - half for rr in range(ROWS_PER_STEP)]),
    )
    for cond, first_row in cases:
        @pl.when(cond)
        def _(first_row=first_row):
            _natten_rows(q_ref, k_refs, v_refs, bias_ref, o_ref, first_row)


def _natten_heads(n_heads):
    return min(MXU_DEPTH // HEAD_DIM, n_heads)


def _natten_bias(rpb, heads):
    h = rpb.shape[0]
    cols = jnp.arange(GRID_W)
    c0 = jnp.clip(cols - NA_COLS // 2, 0, GRID_W - NA_COLS)
    col_mask = (cols[None, :] >= c0[:, None]) & (cols[None, :] < c0[:, None] + NA_COLS)
    dc = jnp.clip(cols[None, :] - cols[:, None], -(NA_COLS - 1), NA_COLS - 1) + NA_COLS - 1
    rpb_cols = rpb.astype(F32)[:, :, dc]
    dr = jnp.arange(NA_ROWS)[:, None] + jnp.arange(NA_ROWS)[None, :]
    tab = rpb_cols[:, dr]
    tab = jnp.where(col_mask[None, None, None], tab, NEG_INF)
    tab = jnp.transpose(tab, (0, 1, 3, 2, 4))
    return tab.reshape(h // heads, heads, NA_ROWS, GRID_W, NA_ROWS * GRID_W)


def _natten(proj, bias_tab, d_hy, d_att):
    bsz, seq, _ = proj.shape
    rows = seq // GRID_W
    nblk = rows // ROWS_PER_STEP
    assert rows % ROWS_PER_STEP == 0 and rows >= NA_ROWS and nblk >= 2
    heads = bias_tab.shape[1]
    width = heads * HEAD_DIM
    assert d_att % width == 0 and (3 * d_hy) % width == 0
    qc, kc, vc = (3 * d_hy) // width, (3 * d_hy + d_att) // width, (3 * d_hy + 2 * d_att) // width
    t = TOK_PER_STEP

    def blk(col, shift):
        return pl.BlockSpec(
            (1, t, width),
            lambda b, p, i: (b, jnp.clip(i + shift, 0, nblk - 1), col + p))

    return pl.pallas_call(
        _natten_kernel,
        grid=(bsz, d_att // width, nblk),
        in_specs=[blk(qc, 0), blk(kc, -1), blk(kc, 0), blk(kc, 1), blk(vc, -1), blk(vc, 0), blk(vc, 1),
                  pl.BlockSpec((1, heads, NA_ROWS, GRID_W, NA_ROWS * GRID_W), lambda b, p, i: (p, 0, 0, 0, 0))],
        out_specs=pl.BlockSpec((1, t, width), lambda b, p, i: (b, i, p)),
        out_shape=jax.ShapeDtypeStruct((bsz, seq, d_att), BF16),
        compiler_params=_cparams(("parallel", "parallel", "arbitrary")),
        name="natten",
    )(proj, proj, proj, proj, proj, proj, proj, bias_tab)


def _outproj_kernel(yh_ref, ya_ref, x_ref, gh_ref, ga_ref, wh_ref, wa_ref, g1_ref, n2_ref, sc_ref, sh_ref,
                    wr_ref, x1_ref, h2_ref, aff_ref):
    def _norm(y_ref, g_ref):
        y = y_ref[0].astype(F32)
        ms = jnp.mean(y * y, axis=-1, keepdims=True)
        return ((y * lax.rsqrt(ms + EPS)) * g_ref[...]).astype(BF16)

    mixed = (jnp.dot(_norm(yh_ref, gh_ref), wh_ref[...], preferred_element_type=F32)
             + jnp.dot(_norm(ya_ref, ga_ref), wa_ref[...], preferred_element_type=F32))
    x1 = x_ref[0] + g1_ref[0] * mixed
    x1_ref[0] = x1
    ms = jnp.mean(x1 * x1, axis=-1, keepdims=True)
    h2 = ((x1 * lax.rsqrt(ms + EPS)) * n2_ref[...]) * (1.0 + sc_ref[0]) + sh_ref[0]
    h2_ref[0] = h2.astype(BF16)
    wr = wr_ref[...]
    w_hi = wr.astype(BF16)
    w_lo = (wr - w_hi.astype(F32)).astype(BF16)
    h_hi = h2.astype(BF16)
    h_lo = (h2 - h_hi.astype(F32)).astype(BF16)
    n_exp = wr.shape[1]
    both = jnp.dot(h_hi, jnp.concatenate([w_hi, w_lo], axis=1), preferred_element_type=F32)
    logits = both[:, :n_exp] + both[:, n_exp:] + jnp.dot(h_lo, w_hi, preferred_element_type=F32)
    m = jnp.max(logits, axis=-1, keepdims=True)
    e = jnp.exp(logits - m)
    aff_ref[0] = e / jnp.sum(e, axis=-1, keepdims=True)


def _outproj(y_hy, y_at, x, gain_hy, gain_att, w_out_bf16, g1, norm2_g, sc2, sh2, w_router):
    bsz, seq, d = x.shape
    d_hy, d_att = y_hy.shape[-1], y_at.shape[-1]
    n_exp = w_router.shape[1]
    tm = _largest_tile(seq, 512, 16)
    row = lambda v: v[:, None, :]
    per_b = pl.BlockSpec((1, 1, d), lambda b, i: (b, 0, 0))
    const = lambda shape: pl.BlockSpec(shape, lambda b, i: tuple(0 for _ in shape))
    return pl.pallas_call(
        _outproj_kernel,
        grid=(bsz, seq // tm),
        in_specs=[
            pl.BlockSpec((1, tm, d_hy), lambda b, i: (b, i, 0)),
            pl.BlockSpec((1, tm, d_att), lambda b, i: (b, i, 0)),
            pl.BlockSpec((1, tm, d), lambda b, i: (b, i, 0)),
            const((1, d_hy)), const((1, d_att)),
            const((d_hy, d)), const((d_att, d)),
            per_b, const((1, d)), per_b, per_b,
            const((d, n_exp)),
        ],
        out_specs=[
            pl.BlockSpec((1, tm, d), lambda b, i: (b, i, 0)),
            pl.BlockSpec((1, tm, d), lambda b, i: (b, i, 0)),
            pl.BlockSpec((1, tm, n_exp), lambda b, i: (b, i, 0)),
        ],
        out_shape=[
            jax.ShapeDtypeStruct((bsz, seq, d), F32),
            jax.ShapeDtypeStruct((bsz, seq, d), BF16),
            jax.ShapeDtypeStruct((bsz, seq, n_exp), F32),
        ],
        compiler_params=_cparams(("parallel", "arbitrary")),
        name="outproj",
    )(y_hy, y_at, x, gain_hy.reshape(1, d_hy), gain_att.reshape(1, d_att),
      w_out_bf16[:d_hy], w_out_bf16[d_hy:], row(g1), norm2_g.reshape(1, d), row(sc2), row(sh2), w_router)


CAST_SPLIT = 8


def _cast_kernel(x_ref, o_ref):
    o_ref[...] = x_ref[...].astype(o_ref.dtype)


def _to_bf16(w):
    n_exp, rows, cols = w.shape
    br = rows // CAST_SPLIT if rows % (CAST_SPLIT * 16) == 0 else rows
    spec = pl.BlockSpec((1, br, cols), lambda e, r: (e, r, 0))
    return pl.pallas_call(
        _cast_kernel,
        grid=(n_exp, rows // br),
        in_specs=[spec],
        out_specs=spec,
        out_shape=jax.ShapeDtypeStruct(w.shape, BF16),
        compiler_params=_cparams(("parallel", "parallel")),
        name="to_bf16",
    )(w)


FF_TILE = 512
OUT_TILE = 512


def _ffn_kernel(x_ref, wg_ref, wu_ref, wd_ref, gt_ref, o_ref, he_ref, *, ff, nf):
    f = pl.program_id(2)

    @pl.when(f < nf)
    def _():
        x = x_ref[0]
        g = jnp.dot(x, wg_ref[0], preferred_element_type=F32)
        u = jnp.dot(x, wu_ref[0], preferred_element_type=F32)
        he = ((g * jax.nn.sigmoid(g)) * u).astype(BF16)
        he_ref[:, pl.ds(pl.multiple_of(f * FF_TILE, FF_TILE), FF_TILE)] = he

    @pl.when(f >= nf)
    def _():
        y = jnp.dot(he_ref[:, :ff], wd_ref[0], preferred_element_type=F32)
        o_ref[0] = (y * gt_ref[0]).astype(o_ref.dtype)


def _expert_ffn(xg, wg, wu, wd, gates):
    n_exp, cap, d = xg.shape
    ff = wg.shape[2]
    tm = _largest_tile(cap, 1024, 16)
    tn = _largest_tile(d, OUT_TILE, LANES)
    nf = pl.cdiv(ff, FF_TILE)
    up_idx = lambda e, m, f: (e, 0, jnp.minimum(f, nf - 1))
    out_idx = lambda e, m, f: (e, m, jnp.maximum(f - nf, 0))
    return pl.pallas_call(
        functools.partial(_ffn_kernel, ff=ff, nf=nf),
        grid=(n_exp, cap // tm, nf + d // tn),
        in_specs=[
            pl.BlockSpec((1, tm, d), lambda e, m, f: (e, m, 0)),
            pl.BlockSpec((1, d, FF_TILE), up_idx),
            pl.BlockSpec((1, d, FF_TILE), up_idx),
            pl.BlockSpec((1, ff, tn), lambda e, m, f: (e, 0, jnp.maximum(f - nf, 0))),
            pl.BlockSpec((1, tm, 1), lambda e, m, f: (e, m, 0)),
        ],
        out_specs=pl.BlockSpec((1, tm, tn), out_idx),
        out_shape=jax.ShapeDtypeStruct((n_exp, cap, d), BF16),
        scratch_shapes=[pltpu.VMEM((tm, nf * FF_TILE), BF16)],
        compiler_params=_cparams(("parallel", "parallel", "arbitrary")),
        name="expert_ffn",
    )(xg, wg, wu, wd, gates[:, :, None])


def _hyena_filter_taps(seq, w0, b0, w1, b1, w2, b2, w3, freq, d_hy):
    pos = jnp.arange(seq, dtype=F32)
    t = jnp.linspace(0.0, 1.0, seq, dtype=F32)[:, None]
    bands = jnp.linspace(1e-4, HY_BANDS - 1, HY_BANDS, dtype=F32)
    ang = (2.0 * math.pi / seq) * pos[:, None] * bands[None, :]
    z = jnp.concatenate([t, jnp.cos(ang), -jnp.sin(ang)], axis=-1)
    hi = lax.Precision.HIGHEST
    h = jnp.sin(freq * (jnp.dot(z, w0, precision=hi) + b0))
    h = jnp.sin(freq * (jnp.dot(h, w1, precision=hi) + b1))
    h = jnp.sin(freq * (jnp.dot(h, w2, precision=hi) + b2))
    h = jnp.dot(h, w3, precision=hi)
    max_decay = math.log(HY_TARGET) / HY_FAST_DECAY
    min_decay = math.log(HY_TARGET) / HY_SLOW_DECAY
    deltas = jnp.abs(jnp.linspace(min_decay, max_decay, d_hy, dtype=F32))
    decay = jnp.exp(-t * deltas[None, :])
    h_fwd = h[:, :d_hy] * decay
    h_bwd = jnp.where(pos[:, None] > 0, h[:, d_hy:] * decay, 0.0)
    norm = jnp.sum(jnp.abs(h_fwd) + jnp.abs(h_bwd), axis=0, keepdims=True)
    return (jnp.stack([h_fwd, h_bwd], axis=0) / norm).astype(BF16)


DFT_N2 = 128
DFT_COLS = 4096
MID_CHANNELS = 256


def _dft_tables(seq):
    n = 2 * seq
    n2 = DFT_N2
    n1 = n // n2
    k1 = jnp.arange(n1, dtype=jnp.int32)
    m1 = jnp.arange(n1 // 2, dtype=jnp.int32)
    ang1 = (2.0 * math.pi / n1) * ((k1[:, None] * m1[None, :]) % n1).astype(F32)
    f1r, f1i = jnp.cos(ang1), -jnp.sin(ang1)
    fwd_a = jnp.concatenate([f1r, f1i], axis=0).astype(BF16)
    inv_a = (jnp.concatenate([f1r.T, f1i.T], axis=1) * (1.0 / n)).astype(BF16)
    k2 = jnp.arange(n2, dtype=jnp.int32)
    ang2 = (2.0 * math.pi / n2) * ((k2[:, None] * k2[None, :]) % n2).astype(F32)
    f2r, f2i = jnp.cos(ang2), -jnp.sin(ang2)
    fwd_c = jnp.block([[f2r, -f2i], [f2i, f2r]]).astype(BF16)
    inv_c = jnp.block([[f2r, f2i], [-f2i, f2r]]).astype(BF16)
    angt = (2.0 * math.pi / n) * (k1[:, None] * k2[None, :]).astype(F32)
    twr, twi = jnp.cos(angt), -jnp.sin(angt)
    return dict(fwd_a=fwd_a, inv_a=inv_a, fwd_c=fwd_c, inv_c=inv_c,
                tw_by_n2=(twr.T[:, :, None], twi.T[:, :, None]),
                tw_by_k1=(twr[:, :, None], twi[:, :, None]))


HALO = 16


def _hy_short_kernel(x1c, x1p, x1n, x2c, x2p, x2n, vc, vp, vn, w_ref, b_ref, vv_ref, x1o_ref):
    i = pl.program_id(1)
    last = pl.num_programs(1) - 1
    tm = x1c.shape[1]
    row = lax.broadcasted_iota(jnp.int32, (tm, 1), 0)

    def conv(cur_ref, prev_ref, next_ref, g):
        cur = cur_ref[0].astype(F32)
        before = jnp.where(i > 0, prev_ref[0, HALO - 1:HALO, :].astype(F32), 0.0)
        after = jnp.where(i < last, next_ref[0, 0:1, :].astype(F32), 0.0)
        up = jnp.where(row == 0, before, pltpu.roll(cur, 1, axis=0))
        dn = jnp.where(row == tm - 1, after, pltpu.roll(cur, tm - 1, axis=0))
        return (w_ref[0, g:g + 1, :] * up + w_ref[1, g:g + 1, :] * cur + w_ref[2, g:g + 1, :] * dn
                + b_ref[g:g + 1, :])

    x1 = conv(x1c, x1p, x1n, 0)
    x2 = conv(x2c, x2p, x2n, 1)
    v = conv(vc, vp, vn, 2)
    vv_ref[0] = (v * x2).astype(vv_ref.dtype)
    x1o_ref[0] = x1.astype(x1o_ref.dtype)


def _hy_short(proj, conv_w, conv_b, d_hy):
    bsz, seq, _ = proj.shape
    tm = _largest_tile(seq, 1024, HALO)
    cb = _largest_tile(d_hy, 512, LANES)
    ncb = d_hy // cb
    nh = seq // HALO

    def specs(g):
        return [
            pl.BlockSpec((1, tm, cb), lambda b, i, c: (b, i, g * ncb + c)),
            pl.BlockSpec((1, HALO, cb), lambda b, i, c: (b, jnp.maximum(i * (tm // HALO) - 1, 0), g * ncb + c)),
            pl.BlockSpec((1, HALO, cb), lambda b, i, c: (b, jnp.minimum((i + 1) * (tm // HALO), nh - 1),
                                                         g * ncb + c)),
        ]

    out_spec = pl.BlockSpec((1, tm, cb), lambda b, i, c: (b, i, c))
    return pl.pallas_call(
        _hy_short_kernel,
        grid=(bsz, seq // tm, ncb),
        in_specs=specs(0) + specs(1) + specs(2) + [
            pl.BlockSpec((3, 3, cb), lambda b, i, c: (0, 0, c)),
            pl.BlockSpec((3, cb), lambda b, i, c: (0, c)),
        ],
        out_specs=[out_spec, out_spec],
        out_shape=[jax.ShapeDtypeStruct((bsz, seq, d_hy), BF16)] * 2,
        compiler_params=_cparams(("parallel", "parallel", "parallel")),
        name="hy_short",
    )(*([proj] * 9), conv_w.reshape(3, 3, d_hy), conv_b.reshape(3, d_hy))


def _hy_fwd_a_kernel(v_ref, fa_ref, twr_ref, twi_ref, o_ref, *, c):
    n1 = fa_ref.shape[0] // 2
    res = jnp.dot(fa_ref[...], v_ref[0], preferred_element_type=F32)
    for s in range(v_ref.shape[2] // c):
        cols = slice(s * c, (s + 1) * c)
        ar, ai = res[:n1, cols], res[n1:, cols]
        tr, ti = twr_ref[s], twi_ref[s]
        o_ref[0, 0, :, cols] = (ar * tr - ai * ti).astype(o_ref.dtype)
        o_ref[0, 1, :, cols] = (ar * ti + ai * tr).astype(o_ref.dtype)


def _hy_fwd_a(vv, tabs):
    bsz, seq, c = vv.shape
    n2 = DFT_N2
    n1 = 2 * seq // n2
    per = max(1, DFT_COLS // c)
    cols = per * c
    twr, twi = tabs["tw_by_n2"]
    return pl.pallas_call(
        functools.partial(_hy_fwd_a_kernel, c=c),
        grid=(bsz, n2 // per),
        in_specs=[
            pl.BlockSpec((1, n1 // 2, cols), lambda b, j: (b, 0, j)),
            pl.BlockSpec((2 * n1, n1 // 2), lambda b, j: (0, 0)),
            pl.BlockSpec((per, n1, 1), lambda b, j: (j, 0, 0)),
            pl.BlockSpec((per, n1, 1), lambda b, j: (j, 0, 0)),
        ],
        out_specs=pl.BlockSpec((1, 2, n1, cols), lambda b, j: (b, 0, 0, j)),
        out_shape=jax.ShapeDtypeStruct((bsz, 2, n1, n2 * c), BF16),
        compiler_params=_cparams(("parallel", "parallel")),
        name="hy_fwd_a",
    )(vv.reshape(bsz, n1 // 2, n2 * c), tabs["fwd_a"], twr, twi)


K1_PER_STEP = 16


def _hy_mid_kernel(a_ref, h_ref, fc_ref, ic_ref, twr_ref, twi_ref, o_ref):
    n2 = DFT_N2
    for k in range(a_ref.shape[2]):
        a = jnp.concatenate([a_ref[0, 0, k], a_ref[0, 1, k]], axis=0)
        x = jnp.dot(fc_ref[...], a, preferred_element_type=F32)
        xr, xi = x[:n2], x[n2:]
        hr, hi = h_ref[0, k].astype(F32), h_ref[1, k].astype(F32)
        y = jnp.concatenate([xr * hr - xi * hi, xr * hi + xi * hr], axis=0).astype(BF16)
        z = jnp.dot(ic_ref[...], y, preferred_element_type=F32)
        zr, zi = z[:n2], z[n2:]
        tr, ti = twr_ref[k], twi_ref[k]
        o_ref[0, 0, k] = (zr * tr + zi * ti).astype(o_ref.dtype)
        o_ref[0, 1, k] = (zi * tr - zr * ti).astype(o_ref.dtype)


def _hy_mid(a, spec_l, tabs):
    bsz, _, n1, n2c = a.shape
    n2 = DFT_N2
    c = n2c // n2
    cb = _largest_tile(c, MID_CHANNELS, LANES)
    kc = _largest_tile(n1, K1_PER_STEP, 1)
    twr, twi = tabs["tw_by_k1"]
    return pl.pallas_call(
        _hy_mid_kernel,
        grid=(c // cb, n1 // kc, bsz),
        in_specs=[
            pl.BlockSpec((1, 2, kc, n2, cb), lambda j, k, b: (b, 0, k, 0, j)),
            pl.BlockSpec((2, kc, n2, cb), lambda j, k, b: (0, k, 0, j)),
            pl.BlockSpec((2 * n2, 2 * n2), lambda j, k, b: (0, 0)),
            pl.BlockSpec((2 * n2, 2 * n2), lambda j, k, b: (0, 0)),
            pl.BlockSpec((kc, n2, 1), lambda j, k, b: (k, 0, 0)),
            pl.BlockSpec((kc, n2, 1), lambda j, k, b: (k, 0, 0)),
        ],
        out_specs=pl.BlockSpec((1, 2, kc, n2, cb), lambda j, k, b: (b, 0, k, 0, j)),
        out_shape=jax.ShapeDtypeStruct((bsz, 2, n1, n2, c), BF16),
        compiler_params=_cparams(("parallel", "parallel", "parallel")),
        name="hy_mid",
    )(a.reshape(bsz, 2, n1, n2, c), spec_l, tabs["fwd_c"], tabs["inv_c"], twr, twi)


def _hy_spec_kernel(a_ref, fc_ref, o_ref):
    n2 = DFT_N2
    cb = a_ref.shape[-1]
    for k in range(a_ref.shape[2]):
        fwd = jnp.concatenate([a_ref[0, 0, k], a_ref[0, 1, k]], axis=0)
        bwd = jnp.concatenate([a_ref[1, 0, k], a_ref[1, 1, k]], axis=0)
        x = jnp.dot(fc_ref[...], jnp.concatenate([fwd, bwd], axis=1), preferred_element_type=F32)
        o_ref[0, k] = (x[:n2, :cb] + x[:n2, cb:]).astype(o_ref.dtype)
        o_ref[1, k] = (x[n2:, :cb] - x[n2:, cb:]).astype(o_ref.dtype)


def _hy_spectrum(taps, tabs):
    a = _hy_fwd_a(taps, tabs)
    _, _, n1, n2c = a.shape
    n2 = DFT_N2
    c = n2c // n2
    cb = _largest_tile(c, MID_CHANNELS, LANES)
    kc = _largest_tile(n1, K1_PER_STEP, 1)
    return pl.pallas_call(
        _hy_spec_kernel,
        grid=(c // cb, n1 // kc),
        in_specs=[
            pl.BlockSpec((2, 2, kc, n2, cb), lambda j, k: (0, 0, k, 0, j)),
            pl.BlockSpec((2 * n2, 2 * n2), lambda j, k: (0, 0)),
        ],
        out_specs=pl.BlockSpec((2, kc, n2, cb), lambda j, k: (0, k, 0, j)),
        out_shape=jax.ShapeDtypeStruct((2, n1, n2, c), BF16),
        compiler_params=_cparams(("parallel", "parallel")),
        name="hy_spectrum",
    )(a.reshape(2, 2, n1, n2, c), tabs["fwd_c"])


def _hy_inv_a_kernel(z_ref, ga_ref, vv_ref, x1_ref, skip_ref, o_ref):
    y = jnp.dot(ga_ref[...], z_ref[0], preferred_element_type=F32)
    v = vv_ref[0].astype(F32)
    o_ref[0] = ((y + skip_ref[...] * v) * x1_ref[0].astype(F32)).astype(o_ref.dtype)


def _hy_inv_a(z, vv, x1c, skip, tabs):
    bsz, seq, c = vv.shape
    n2 = DFT_N2
    n1 = 2 * seq // n2
    per = max(1, DFT_COLS // c)
    cols = per * c
    flat = lambda t: t.reshape(bsz, n1 // 2, n2 * c)
    data = pl.BlockSpec((1, n1 // 2, cols), lambda b, j: (b, 0, j))
    out = pl.pallas_call(
        _hy_inv_a_kernel,
        grid=(bsz, n2 // per),
        in_specs=[
            pl.BlockSpec((1, 2 * n1, cols), lambda b, j: (b, 0, j)),
            pl.BlockSpec((n1 // 2, 2 * n1), lambda b, j: (0, 0)),
            data, data,
            pl.BlockSpec((1, cols), lambda b, j: (0, 0)),
        ],
        out_specs=data,
        out_shape=jax.ShapeDtypeStruct((bsz, n1 // 2, n2 * c), BF16),
        compiler_params=_cparams(("parallel", "parallel")),
        name="hy_inv_a",
    )(z.reshape(bsz, 2 * n1, n2 * c), tabs["inv_a"], flat(vv), flat(x1c), jnp.tile(skip, per).reshape(1, cols))
    return out.reshape(bsz, seq, c)


def _hyena_mixer(proj, conv_w, conv_b, spec_l, skip, tabs, d_hy):
    vv, x1c = _hy_short(proj, conv_w, conv_b, d_hy)
    a = _hy_fwd_a(vv, tabs)
    z = _hy_mid(a, spec_l, tabs)
    return _hy_inv_a(z, vv, x1c, skip, tabs)


def _route_kernel(a_ref, upper_ref, lower_ref, sel_ref, idx_ref, slot_scr, *, cap):
    a = a_ref[0]
    n_rows = a.shape[0]
    bits = pltpu.bitcast(a, jnp.int32)
    thr = jnp.int32(0)
    for b in range(30, -1, -1):
        cand = thr | jnp.int32(1 << b)
        cnt = jnp.sum(jnp.where(bits >= cand, 1.0, 0.0))
        thr = jnp.where(cnt >= cap, cand, thr)
    gt = jnp.where(bits > thr, 1.0, 0.0)
    eq = jnp.where(bits == thr, 1.0, 0.0)
    need = cap - jnp.sum(gt)

    def prefix(m):
        within = jnp.dot(m.astype(BF16), upper_ref[...], preferred_element_type=F32)
        totals = jnp.broadcast_to(within[:, LANES - 1:LANES], within.shape).astype(BF16)
        return within + jnp.dot(lower_ref[...], totals, preferred_element_type=F32)

    sel = gt + eq * jnp.where(prefix(eq) <= need, 1.0, 0.0)
    slot = jnp.where(sel > 0.0, prefix(sel) - 1.0, -1.0)
    sel_ref[0] = slot.astype(jnp.int32)
    slot_scr[...] = slot

    out_rows = idx_ref.shape[1]
    row_id = lax.broadcasted_iota(jnp.int32, (out_rows, 2 * LANES), 0).astype(F32)
    col_id = lax.broadcasted_iota(jnp.int32, (LANES, 2 * LANES), 0).astype(F32)
    lane = lax.broadcasted_iota(jnp.int32, (1, LANES), 1).astype(F32)
    lanes2 = jnp.concatenate([lane, lane], axis=1)

    def body(g, acc):
        pair = slot_scr[pl.ds(2 * g, 2), :]
        s = jnp.concatenate([pair[0:1], pair[1:2]], axis=1)
        hi = jnp.floor(s * (1.0 / LANES))
        lo = s - hi * LANES
        base = jnp.asarray(2 * g).astype(F32)
        rows = jnp.concatenate([jnp.full((1, LANES), 0.0, F32), jnp.full((1, LANES), 1.0, F32)], axis=1) + base
        in_row = hi == row_id
        lhs = jnp.concatenate([jnp.where(in_row, rows, 0.0), jnp.where(in_row, lanes2, 0.0)], axis=0)
        rhs = jnp.where(lo == col_id, 1.0, 0.0)
        return acc + lax.dot_general(lhs.astype(BF16), rhs.astype(BF16), (((1,), (1,)), ((), ())),
                                     preferred_element_type=F32)

    acc = lax.fori_loop(0, n_rows // 2, body, jnp.zeros((2 * out_rows, LANES), F32), unroll=4)
    idx_ref[0] = (acc[:out_rows] * LANES + acc[out_rows:]).astype(jnp.int32)


def _route(aff_t, cap):
    n_exp, n_tok = aff_t.shape
    n_rows = n_tok // LANES
    assert n_tok % (2 * LANES) == 0 and cap % LANES == 0
    assert n_rows <= BF16_EXACT_INT
    i = jnp.arange(LANES)
    upper = (i[:, None] <= i[None, :]).astype(BF16)
    r = jnp.arange(n_rows)
    lower = (r[:, None] > r[None, :]).astype(BF16)
    sel, idx = pl.pallas_call(
        functools.partial(_route_kernel, cap=cap),
        grid=(n_exp,),
        in_specs=[
            pl.BlockSpec((1, n_rows, LANES), lambda e: (e, 0, 0)),
            pl.BlockSpec((LANES, LANES), lambda e: (0, 0)),
            pl.BlockSpec((n_rows, n_rows), lambda e: (0, 0)),
        ],
        out_specs=[
            pl.BlockSpec((1, n_rows, LANES), lambda e: (e, 0, 0)),
            pl.BlockSpec((1, cap // LANES, LANES), lambda e: (e, 0, 0)),
        ],
        out_shape=[
            jax.ShapeDtypeStruct((n_exp, n_rows, LANES), jnp.int32),
            jax.ShapeDtypeStruct((n_exp, cap // LANES, LANES), jnp.int32),
        ],
        scratch_shapes=[pltpu.VMEM((n_rows, LANES), F32)],
        compiler_params=_cparams(("parallel",)),
        name="route",
    )(aff_t.reshape(n_exp, n_rows, LANES), upper, lower)
    return sel.reshape(n_exp, n_tok), idx.reshape(n_exp, cap)


COMBINE_TOKENS = 512
WINDOW = 128
ROW_ALIGN = 16


def _combine_kernel(lo_ref, nwin_ref, slot_ref, x1_ref, g2_ref, ye_hbm, o_ref, buf, acc_ref, sem,
                    *, n_exp, cap):
    i = pl.program_id(0)
    n_tiles = pl.num_programs(0)

    def window_start(tile, e, k):
        return pl.multiple_of(jnp.minimum(lo_ref[tile * n_exp + e] + k * WINDOW, cap - WINDOW), ROW_ALIGN)

    def copy(tile, e, k, slot):
        return pltpu.make_async_copy(ye_hbm.at[e, pl.ds(window_start(tile, e, k), WINDOW), :],
                                     buf.at[slot, pl.ds(e * WINDOW, WINDOW), :], sem.at[slot])

    def fetch(tile, k, slot):
        for e in range(n_exp):
            copy(tile, e, k, slot).start()

    def wait(tile, k, slot):
        for e in range(n_exp):
            copy(tile, e, k, slot).wait()

    def placed(k, slot):
        slots = slot_ref[...]
        j = lax.broadcasted_iota(jnp.int32, (slots.shape[0], WINDOW), 1)
        cols = []
        for e in range(n_exp):
            s = slots[:, e:e + 1]
            lo = lo_ref[i * n_exp + e] + k * WINDOW
            row = jnp.where(s >= lo, s - window_start(i, e, k), -1)
            cols.append(jnp.where(row == j, 1.0, 0.0).astype(BF16))
        return jnp.dot(jnp.concatenate(cols, axis=1), buf[slot], preferred_element_type=F32)

    cur = lax.rem(i, 2)

    @pl.when(i == 0)
    def _():
        fetch(0, 0, 0)

    @pl.when(i + 1 < n_tiles)
    def _():
        fetch(i + 1, 0, 1 - cur)

    wait(i, 0, cur)
    acc_ref[...] = placed(0, cur)

    def extra(k, carry):
        fetch(i, k, 2)
        wait(i, k, 2)
        acc_ref[...] += placed(k, 2)
        return carry

    lax.fori_loop(1, nwin_ref[i], extra, 0)
    o_ref[...] = x1_ref[...] + g2_ref[0] * acc_ref[...]


def _combine(x1, g2, ye, slot):
    bsz, seq, d = x1.shape
    n_exp, cap, _ = ye.shape
    n_tok = bsz * seq
    t = _largest_tile(seq, COMBINE_TOKENS, ROW_ALIGN)
    n_tiles = n_tok // t
    assert cap >= WINDOW and cap % ROW_ALIGN == 0
    cnt = (slot >= 0).astype(jnp.int32).reshape(n_exp, n_tiles, t).sum(axis=2)
    first = jnp.cumsum(cnt, axis=1) - cnt
    lo = (first // ROW_ALIGN) * ROW_ALIGN
    nwin = jnp.maximum(jnp.max(-(-(first - lo + cnt) // WINDOW), axis=0), 1)
    tiles_per_seq = seq // t
    grid_spec = pltpu.PrefetchScalarGridSpec(
        num_scalar_prefetch=2,
        grid=(n_tiles,),
        in_specs=[
            pl.BlockSpec((t, n_exp), lambda i, lo_r, nw_r: (i, 0)),
            pl.BlockSpec((t, d), lambda i, lo_r, nw_r: (i, 0)),
            pl.BlockSpec((1, 1, d), lambda i, lo_r, nw_r: (i // tiles_per_seq, 0, 0)),
            pl.BlockSpec(memory_space=pl.ANY),
        ],
        out_specs=pl.BlockSpec((t, d), lambda i, lo_r, nw_r: (i, 0)),
        scratch_shapes=[
            pltpu.VMEM((3, n_exp * WINDOW, d), BF16),
            pltpu.VMEM((t, d), F32),
            pltpu.SemaphoreType.DMA((3,)),
        ],
    )
    out = pl.pallas_call(
        functools.partial(_combine_kernel, n_exp=n_exp, cap=cap),
        grid_spec=grid_spec,
        out_shape=jax.ShapeDtypeStruct((n_tok, d), F32),
        compiler_params=_cparams(("arbitrary",)),
        name="combine",
    )(lo.T.reshape(-1).astype(jnp.int32), nwin.astype(jnp.int32), slot.T.astype(jnp.int32),
      x1.reshape(n_tok, d), g2[:, None, :], ye)
    return out.reshape(bsz, seq, d)


def _layer(x, ada, p):
    bsz, seq, d = x.shape
    d_hy = p["hy_skip"].shape[0]
    d_att = p["out_norm_att"].shape[0]
    n_exp = p["w_router"].shape[1]
    sh1, sc1, g1, sh2, sc2, g2 = jnp.split(ada, N_ADA, axis=-1)

    q_gain = p["q_norm_g"] * (HEAD_DIM ** -0.5)
    proj = _inproj(x, p["norm1_g"], sc1, sh1, p["w_in_bf16"], q_gain, p["k_norm_g"], d_hy, d_att)

    y_hy = _hyena_mixer(proj, p["hy_conv_w"], p["hy_conv_b"], p["spec"], p["hy_skip"], p["dft"], d_hy)
    y_at = _natten(proj, p["bias_tab"], d_hy, d_att)

    x1, h2, aff = _outproj(y_hy, y_at, x, p["out_norm_hy"], p["out_norm_att"], p["w_out_bf16"],
                           g1, p["norm2_g"], sc2, sh2, p["w_router"])

    n_tok = bsz * seq
    cap = CAPACITY_FACTOR * n_tok // n_exp
    aff_t = aff.reshape(n_tok, n_exp).T
    slot, idx = _route(aff_t, cap)
    gates = jnp.take_along_axis(aff_t, idx, axis=1)
    xg = jnp.take(h2.reshape(n_tok, d), idx, axis=0)
    ye = _expert_ffn(xg, p["wg"], p["wu"], p["wd"], gates)
    return _combine(x1, g2, ye, slot)


def kernel(x_prompt, x_sample, c_prompt, c_sample, ada_w, ada_b, norm1_g, w_in, hy_conv_w, hy_conv_b, hy_f_w0, hy_f_b0, hy_f_w1, hy_f_b1, hy_f_w2, hy_f_b2, hy_f_w3, hy_f_freq, hy_skip, q_norm_g, k_norm_g, rpb, out_norm_hy, out_norm_att, w_out, norm2_g, w_router, w_gate, w_up, w_down):
    depth = ada_w.shape[0]
    y_prompt, y_sample = x_prompt, x_sample
    nbp = x_prompt.shape[0]
    assert x_prompt.shape[1] == x_sample.shape[1]
    seq = x_prompt.shape[1]
    for l in range(depth):
        d_hy = hy_skip.shape[-1]
        dft = _dft_tables(seq)
        taps = _hyena_filter_taps(seq, hy_f_w0[l], hy_f_b0[l], hy_f_w1[l], hy_f_b1[l], hy_f_w2[l],
                                  hy_f_b2[l], hy_f_w3[l], hy_f_freq[l], d_hy)
        p = {
            "norm1_g": norm1_g[l], "w_in_bf16": w_in[l].astype(BF16),
            "hy_conv_w": hy_conv_w[l], "hy_conv_b": hy_conv_b[l], "hy_skip": hy_skip[l],
            "spec": _hy_spectrum(taps, dft), "dft": dft,
            "q_norm_g": q_norm_g[l], "k_norm_g": k_norm_g[l], "bias_tab": _natten_bias(rpb[l], _natten_heads(rpb.shape[1])),
            "out_norm_hy": out_norm_hy[l], "out_norm_att": out_norm_att[l],
            "w_out_bf16": w_out[l].astype(BF16), "norm2_g": norm2_g[l], "w_router": w_router[l],
            "wg": _to_bf16(w_gate[l]), "wu": _to_bf16(w_up[l]), "wd": _to_bf16(w_down[l]),
        }
        ada = _ada(jnp.concatenate([c_prompt, c_sample], axis=0), ada_w[l], ada_b[l])
        y_prompt = _layer(y_prompt, ada[:nbp], p)
        y_sample = _layer(y_sample, ada[nbp:], p)
    return (y_prompt, y_sample)
```

```python
import functools
import math

import jax
import jax.numpy as jnp
from jax import lax
from jax.experimental import pallas as pl
from jax.experimental.pallas import tpu as pltpu

F32 = jnp.float32
BF16 = jnp.bfloat16

HEAD_DIM = 64
GRID_W = 64
NA_ROWS = 8
NA_COLS = 16
HY_BANDS = 16
HY_TARGET = 1e-2
HY_FAST_DECAY = 0.3
HY_SLOW_DECAY = 1.5
CAPACITY_FACTOR = 2
N_ADA = 6
EPS = 1e-6
NEG_INF = -1e30

LANES = 128
MXU_DEPTH = 256
BF16_EXACT_INT = 256
VMEM_LIMIT_BYTES = 56 * 1024 * 1024

ROWS_PER_STEP = 8
TOK_PER_STEP = ROWS_PER_STEP * GRID_W


def _largest_tile(n, pref, align):
    if n <= pref:
        return n
    t = (pref // align) * align
    while t > align and n % t:
        t -= align
    assert n % t == 0, (n, pref, align)
    return t


def _cparams(sem):
    return pltpu.CompilerParams(dimension_semantics=sem, vmem_limit_bytes=VMEM_LIMIT_BYTES)


def _ada_kernel(ct_ref, w_ref, b_ref, o_ref):
    ct = ct_ref[...]
    st = ct * jax.nn.sigmoid(ct)
    w = w_ref[...]
    for b in range(ct.shape[1]):
        o_ref[b:b + 1, :] = jnp.sum(w * st[:, b:b + 1], axis=0, keepdims=True) + b_ref[...]


def _ada(c, ada_w, ada_b):
    nb, d = c.shape
    n = ada_w.shape[1]
    tn = _largest_tile(n, 1024, LANES)
    return pl.pallas_call(
        _ada_kernel,
        grid=(n // tn,),
        in_specs=[
            pl.BlockSpec((d, nb), lambda j: (0, 0)),
            pl.BlockSpec((d, tn), lambda j: (0, j)),
            pl.BlockSpec((1, tn), lambda j: (0, j)),
        ],
        out_specs=pl.BlockSpec((nb, tn), lambda j: (0, j)),
        out_shape=jax.ShapeDtypeStruct((nb, n), F32),
        compiler_params=_cparams(("arbitrary",)),
        name="ada",
    )(c.T, ada_w, ada_b.reshape(1, n))


def _inproj_kernel(x_ref, g_ref, sc_ref, sh_ref, w_ref, qg_ref, kg_ref, bd_ref, o_ref, h_scr,
                   *, tn, q_col0, k_col0, v_col0):
    j = pl.program_id(2)

    @pl.when(j == 0)
    def _():
        x = x_ref[0]
        ms = jnp.mean(x * x, axis=-1, keepdims=True)
        y = (x * lax.rsqrt(ms + EPS)) * g_ref[...]
        h_scr[...] = (y * (1.0 + sc_ref[0]) + sh_ref[0]).astype(BF16)

    col0 = j * tn
    is_q = jnp.logical_and(col0 >= q_col0, col0 < k_col0)
    is_k = jnp.logical_and(col0 >= k_col0, col0 < v_col0)
    is_qk = jnp.logical_or(is_q, is_k)

    @pl.when(jnp.logical_not(is_qk))
    def _():
        o_ref[0] = jnp.dot(h_scr[...], w_ref[...], preferred_element_type=F32).astype(o_ref.dtype)

    @pl.when(is_qk)
    def _():
        acc = jnp.dot(h_scr[...], w_ref[...], preferred_element_type=F32)
        gain = jnp.where(is_q, qg_ref[...], kg_ref[...])
        bd = bd_ref[...]
        for c in range(tn // LANES):
            sl = slice(c * LANES, (c + 1) * LANES)
            a = acc[:, sl]
            sq = a * a
            hi = sq.astype(BF16)
            lo = (sq - hi.astype(F32)).astype(BF16)
            ss = (jnp.dot(hi, bd, preferred_element_type=F32)
                  + jnp.dot(lo, bd, preferred_element_type=F32))
            y = (a * lax.rsqrt(ss * (1.0 / HEAD_DIM) + EPS)) * gain[:, sl]
            o_ref[0, :, sl] = y.astype(o_ref.dtype)


def _inproj(x, norm_g, sc, sh, w_bf16, q_gain_row, k_gain_row, d_hy, d_att):
    bsz, seq, d = x.shape
    n = w_bf16.shape[1]
    tm = _largest_tile(seq, 1024, 16)
    tn = _largest_tile(d_att, 1024, LANES)
    assert (3 * d_hy) % tn == 0 and n % tn == 0
    lane = jnp.arange(LANES)
    bd = (lane[:, None] // HEAD_DIM == lane[None, :] // HEAD_DIM).astype(BF16)
    kern = functools.partial(_inproj_kernel, tn=tn, q_col0=3 * d_hy, k_col0=3 * d_hy + d_att,
                             v_col0=3 * d_hy + 2 * d_att)
    return pl.pallas_call(
        kern,
        grid=(bsz, seq // tm, n // tn),
        in_specs=[
            pl.BlockSpec((1, tm, d), lambda b, i, j: (b, i, 0)),
            pl.BlockSpec((1, d), lambda b, i, j: (0, 0)),
            pl.BlockSpec((1, 1, d), lambda b, i, j: (b, 0, 0)),
            pl.BlockSpec((1, 1, d), lambda b, i, j: (b, 0, 0)),
            pl.BlockSpec((d, tn), lambda b, i, j: (0, j)),
            pl.BlockSpec((1, tn), lambda b, i, j: (0, 0)),
            pl.BlockSpec((1, tn), lambda b, i, j: (0, 0)),
            pl.BlockSpec((LANES, LANES), lambda b, i, j: (0, 0)),
        ],
        out_specs=pl.BlockSpec((1, tm, tn), lambda b, i, j: (b, i, j)),
        out_shape=jax.ShapeDtypeStruct((bsz, seq, n), BF16),
        scratch_shapes=[pltpu.VMEM((tm, d), BF16)],
        compiler_params=_cparams(("parallel", "parallel", "arbitrary")),
        name="inproj",
    )(x, norm_g.reshape(1, d), sc[:, None, :], sh[:, None, :], w_bf16,
      jnp.tile(q_gain_row, tn // HEAD_DIM).reshape(1, tn),
      jnp.tile(k_gain_row, tn // HEAD_DIM).reshape(1, tn), bd)


def _natten_rows(q_ref, k_refs, v_refs, bias_ref, o_ref, first_row):
    width = q_ref.shape[-1]
    heads = width // HEAD_DIM
    lane_head = lax.broadcasted_iota(jnp.int32, (1, width), 1) // HEAD_DIM
    zero = jnp.zeros((), BF16)

    def window(refs, row):
        blk, o = divmod(row, ROWS_PER_STEP)
        if o == 0:
            return refs[blk][0]
        return jnp.concatenate([refs[blk][0, o * GRID_W:, :], refs[blk + 1][0, :o * GRID_W, :]], axis=0)

    for rr in range(ROWS_PER_STEP):
        row = first_row[rr]
        start = row - ROWS_PER_STEP - rr + NA_ROWS - 1
        kw = window(k_refs, row)
        vw = window(v_refs, row)
        q = q_ref[0, rr * GRID_W:(rr + 1) * GRID_W, :]
        qs = jnp.concatenate([jnp.where(lane_head == h, q, zero) for h in range(heads)], axis=0)
        s = lax.dot_general(qs, kw, (((1,), (1,)), ((), ())), preferred_element_type=F32)
        s = s + jnp.concatenate([bias_ref[0, h, start] for h in range(heads)], axis=0)
        m = jnp.max(s, axis=-1, keepdims=True)
        p = jnp.exp(s - m)
        l = jnp.sum(p, axis=-1, keepdims=True)
        o_all = jnp.dot(p.astype(BF16), vw, preferred_element_type=F32) / l
        o = o_all[:GRID_W]
        for h in range(1, heads):
            o = jnp.where(lane_head == h, o_all[h * GRID_W:(h + 1) * GRID_W], o)
        o_ref[0, rr * GRID_W:(rr + 1) * GRID_W, :] = o.astype(o_ref.dtype)


def _natten_kernel(q_ref, k0_ref, k1_ref, k2_ref, v0_ref, v1_ref, v2_ref, bias_ref, o_ref):
    i = pl.program_id(2)
    last = pl.num_programs(2) - 1
    k_refs, v_refs = (k0_ref, k1_ref, k2_ref), (v0_ref, v1_ref, v2_ref)
    half = NA_ROWS // 2
    cases = (
        (i == 0, [ROWS_PER_STEP + max(rr - half, 0) for rr in range(ROWS_PER_STEP)]),
        (i == last, [min(rr + ROWS_PER_STEP - half, ROWS_PER_STEP) for rr in range(ROWS_PER_STEP)]),
        (jnp.logical_and(i > 0, i < last), [rr + ROWS_PER_STEP - half for rr in range(ROWS_PER_STEP)]),
    )
    for cond, first_row in cases:
        @pl.when(cond)
        def _(first_row=first_row):
            _natten_rows(q_ref, k_refs, v_refs, bias_ref, o_ref, first_row)


def _natten_heads(n_heads):
    return min(MXU_DEPTH // HEAD_DIM, n_heads)


def _natten_bias(rpb, heads):
    h = rpb.shape[0]
    cols = jnp.arange(GRID_W)
    c0 = jnp.clip(cols - NA_COLS // 2, 0, GRID_W - NA_COLS)
    col_mask = (cols[None, :] >= c0[:, None]) & (cols[None, :] < c0[:, None] + NA_COLS)
    dc = jnp.clip(cols[None, :] - cols[:, None], -(NA_COLS - 1), NA_COLS - 1) + NA_COLS - 1
    rpb_cols = rpb.astype(F32)[:, :, dc]
    dr = jnp.arange(NA_ROWS)[:, None] + jnp.arange(NA_ROWS)[None, :]
    tab = rpb_cols[:, dr]
    tab = jnp.where(col_mask[None, None, None], tab, NEG_INF)
    tab = jnp.transpose(tab, (0, 1, 3, 2, 4))
    return tab.reshape(h // heads, heads, NA_ROWS, GRID_W, NA_ROWS * GRID_W)


def _natten(proj, bias_tab, d_hy, d_att):
    bsz, seq, _ = proj.shape
    rows = seq // GRID_W
    nblk = rows // ROWS_PER_STEP
    assert rows % ROWS_PER_STEP == 0 and rows >= NA_ROWS and nblk >= 2
    heads = bias_tab.shape[1]
    width = heads * HEAD_DIM
    assert d_att % width == 0 and (3 * d_hy) % width == 0
    qc, kc, vc = (3 * d_hy) // width, (3 * d_hy + d_att) // width, (3 * d_hy + 2 * d_att) // width
    t = TOK_PER_STEP

    def blk(col, shift):
        return pl.BlockSpec(
            (1, t, width),
            lambda b, p, i: (b, jnp.clip(i + shift, 0, nblk - 1), col + p))

    return pl.pallas_call(
        _natten_kernel,
        grid=(bsz, d_att // width, nblk),
        in_specs=[blk(qc, 0), blk(kc, -1), blk(kc, 0), blk(kc, 1), blk(vc, -1), blk(vc, 0), blk(vc, 1),
                  pl.BlockSpec((1, heads, NA_ROWS, GRID_W, NA_ROWS * GRID_W), lambda b, p, i: (p, 0, 0, 0, 0))],
        out_specs=pl.BlockSpec((1, t, width), lambda b, p, i: (b, i, p)),
        out_shape=jax.ShapeDtypeStruct((bsz, seq, d_att), BF16),
        compiler_params=_cparams(("parallel", "parallel", "arbitrary")),
        name="natten",
    )(proj, proj, proj, proj, proj, proj, proj, bias_tab)


def _outproj_kernel(yh_ref, ya_ref, x_ref, gh_ref, ga_ref, wh_ref, wa_ref, g1_ref, n2_ref, sc_ref, sh_ref,
                    wr_ref, x1_ref, h2_ref, aff_ref):
    def _norm(y_ref, g_ref):
        y = y_ref[0].astype(F32)
        ms = jnp.mean(y * y, axis=-1, keepdims=True)
        return ((y * lax.rsqrt(ms + EPS)) * g_ref[...]).astype(BF16)

    mixed = (jnp.dot(_norm(yh_ref, gh_ref), wh_ref[...], preferred_element_type=F32)
             + jnp.dot(_norm(ya_ref, ga_ref), wa_ref[...], preferred_element_type=F32))
    x1 = x_ref[0] + g1_ref[0] * mixed
    x1_ref[0] = x1
    ms = jnp.mean(x1 * x1, axis=-1, keepdims=True)
    h2 = ((x1 * lax.rsqrt(ms + EPS)) * n2_ref[...]) * (1.0 + sc_ref[0]) + sh_ref[0]
    h2_ref[0] = h2.astype(BF16)
    wr = wr_ref[...]
    w_hi = wr.astype(BF16)
    w_lo = (wr - w_hi.astype(F32)).astype(BF16)
    h_hi = h2.astype(BF16)
    h_lo = (h2 - h_hi.astype(F32)).astype(BF16)
    n_exp = wr.shape[1]
    both = jnp.dot(h_hi, jnp.concatenate([w_hi, w_lo], axis=1), preferred_element_type=F32)
    logits = both[:, :n_exp] + both[:, n_exp:] + jnp.dot(h_lo, w_hi, preferred_element_type=F32)
    m = jnp.max(logits, axis=-1, keepdims=True)
    e = jnp.exp(logits - m)
    aff_ref[0] = e / jnp.sum(e, axis=-1, keepdims=True)


def _outproj(y_hy, y_at, x, gain_hy, gain_att, w_out_bf16, g1, norm2_g, sc2, sh2, w_router):
    bsz, seq, d = x.shape
    d_hy, d_att = y_hy.shape[-1], y_at.shape[-1]
    n_exp = w_router.shape[1]
    tm = _largest_tile(seq, 512, 16)
    row = lambda v: v[:, None, :]
    per_b = pl.BlockSpec((1, 1, d), lambda b, i: (b, 0, 0))
    const = lambda shape: pl.BlockSpec(shape, lambda b, i: tuple(0 for _ in shape))
    return pl.pallas_call(
        _outproj_kernel,
        grid=(bsz, seq // tm),
        in_specs=[
            pl.BlockSpec((1, tm, d_hy), lambda b, i: (b, i, 0)),
            pl.BlockSpec((1, tm, d_att), lambda b, i: (b, i, 0)),
            pl.BlockSpec((1, tm, d), lambda b, i: (b, i, 0)),
            const((1, d_hy)), const((1, d_att)),
            const((d_hy, d)), const((d_att, d)),
            per_b, const((1, d)), per_b, per_b,
            const((d, n_exp)),
        ],
        out_specs=[
            pl.BlockSpec((1, tm, d), lambda b, i: (b, i, 0)),
            pl.BlockSpec((1, tm, d), lambda b, i: (b, i, 0)),
            pl.BlockSpec((1, tm, n_exp), lambda b, i: (b, i, 0)),
        ],
        out_shape=[
            jax.ShapeDtypeStruct((bsz, seq, d), F32),
            jax.ShapeDtypeStruct((bsz, seq, d), BF16),
            jax.ShapeDtypeStruct((bsz, seq, n_exp), F32),
        ],
        compiler_params=_cparams(("parallel", "arbitrary")),
        name="outproj",
    )(y_hy, y_at, x, gain_hy.reshape(1, d_hy), gain_att.reshape(1, d_att),
      w_out_bf16[:d_hy], w_out_bf16[d_hy:], row(g1), norm2_g.reshape(1, d), row(sc2), row(sh2), w_router)


CAST_SPLIT = 8


def _cast_kernel(x_ref, o_ref):
    o_ref[...] = x_ref[...].astype(o_ref.dtype)


def _to_bf16(w):
    n_exp, rows, cols = w.shape
    br = rows // CAST_SPLIT if rows % (CAST_SPLIT * 16) == 0 else rows
    spec = pl.BlockSpec((1, br, cols), lambda e, r: (e, r, 0))
    return pl.pallas_call(
        _cast_kernel,
        grid=(n_exp, rows // br),
        in_specs=[spec],
        out_specs=spec,
        out_shape=jax.ShapeDtypeStruct(w.shape, BF16),
        compiler_params=_cparams(("parallel", "parallel")),
        name="to_bf16",
    )(w)


FF_TILE = 512
OUT_TILE = 512


def _ffn_kernel(x_ref, wg_ref, wu_ref, wd_ref, gt_ref, o_ref, he_ref, *, ff, nf):
    f = pl.program_id(2)

    @pl.when(f < nf)
    def _():
        x = x_ref[0]
        g = jnp.dot(x, wg_ref[0], preferred_element_type=F32)
        u = jnp.dot(x, wu_ref[0], preferred_element_type=F32)
        he = ((g * jax.nn.sigmoid(g)) * u).astype(BF16)
        he_ref[:, pl.ds(pl.multiple_of(f * FF_TILE, FF_TILE), FF_TILE)] = he

    @pl.when(f >= nf)
    def _():
        y = jnp.dot(he_ref[:, :ff], wd_ref[0], preferred_element_type=F32)
        o_ref[0] = (y * gt_ref[0]).astype(o_ref.dtype)


def _expert_ffn(xg, wg, wu, wd, gates):
    n_exp, cap, d = xg.shape
    ff = wg.shape[2]
    tm = _largest_tile(cap, 1024, 16)
    tn = _largest_tile(d, OUT_TILE, LANES)
    nf = pl.cdiv(ff, FF_TILE)
    up_idx = lambda e, m, f: (e, 0, jnp.minimum(f, nf - 1))
    out_idx = lambda e, m, f: (e, m, jnp.maximum(f - nf, 0))
    return pl.pallas_call(
        functools.partial(_ffn_kernel, ff=ff, nf=nf),
        grid=(n_exp, cap // tm, nf + d // tn),
        in_specs=[
            pl.BlockSpec((1, tm, d), lambda e, m, f: (e, m, 0)),
            pl.BlockSpec((1, d, FF_TILE), up_idx),
            pl.BlockSpec((1, d, FF_TILE), up_idx),
            pl.BlockSpec((1, ff, tn), lambda e, m, f: (e, 0, jnp.maximum(f - nf, 0))),
            pl.BlockSpec((1, tm, 1), lambda e, m, f: (e, m, 0)),
        ],
        out_specs=pl.BlockSpec((1, tm, tn), out_idx),
        out_shape=jax.ShapeDtypeStruct((n_exp, cap, d), BF16),
        scratch_shapes=[pltpu.VMEM((tm, nf * FF_TILE), BF16)],
        compiler_params=_cparams(("parallel", "parallel", "arbitrary")),
        name="expert_ffn",
    )(xg, wg, wu, wd, gates[:, :, None])


def _hyena_filter_taps(seq, w0, b0, w1, b1, w2, b2, w3, freq, d_hy):
    pos = jnp.arange(seq, dtype=F32)
    t = jnp.linspace(0.0, 1.0, seq, dtype=F32)[:, None]
    bands = jnp.linspace(1e-4, HY_BANDS - 1, HY_BANDS, dtype=F32)
    ang = (2.0 * math.pi / seq) * pos[:, None] * bands[None, :]
    z = jnp.concatenate([t, jnp.cos(ang), -jnp.sin(ang)], axis=-1)
    hi = lax.Precision.HIGHEST
    h = jnp.sin(freq * (jnp.dot(z, w0, precision=hi) + b0))
    h = jnp.sin(freq * (jnp.dot(h, w1, precision=hi) + b1))
    h = jnp.sin(freq * (jnp.dot(h, w2, precision=hi) + b2))
    h = jnp.dot(h, w3, precision=hi)
    max_decay = math.log(HY_TARGET) / HY_FAST_DECAY
    min_decay = math.log(HY_TARGET) / HY_SLOW_DECAY
    deltas = jnp.abs(jnp.linspace(min_decay, max_decay, d_hy, dtype=F32))
    decay = jnp.exp(-t * deltas[None, :])
    h_fwd = h[:, :d_hy] * decay
    h_bwd = jnp.where(pos[:, None] > 0, h[:, d_hy:] * decay, 0.0)
    norm = jnp.sum(jnp.abs(h_fwd) + jnp.abs(h_bwd), axis=0, keepdims=True)
    return (jnp.stack([h_fwd, h_bwd], axis=0) / norm).astype(BF16)


DFT_N2 = 128
DFT_COLS = 4096
MID_CHANNELS = 256


def _dft_tables(seq):
    n = 2 * seq
    n2 = DFT_N2
    n1 = n // n2
    k1 = jnp.arange(n1, dtype=jnp.int32)
    m1 = jnp.arange(n1 // 2, dtype=jnp.int32)
    ang1 = (2.0 * math.pi / n1) * ((k1[:, None] * m1[None, :]) % n1).astype(F32)
    f1r, f1i = jnp.cos(ang1), -jnp.sin(ang1)
    fwd_a = jnp.concatenate([f1r, f1i], axis=0).astype(BF16)
    inv_a = (jnp.concatenate([f1r.T, f1i.T], axis=1) * (1.0 / n)).astype(BF16)
    k2 = jnp.arange(n2, dtype=jnp.int32)
    ang2 = (2.0 * math.pi / n2) * ((k2[:, None] * k2[None, :]) % n2).astype(F32)
    f2r, f2i = jnp.cos(ang2), -jnp.sin(ang2)
    fwd_c = jnp.block([[f2r, -f2i], [f2i, f2r]]).astype(BF16)
    inv_c = jnp.block([[f2r, f2i], [-f2i, f2r]]).astype(BF16)
    angt = (2.0 * math.pi / n) * (k1[:, None] * k2[None, :]).astype(F32)
    twr, twi = jnp.cos(angt), -jnp.sin(angt)
    return dict(fwd_a=fwd_a, inv_a=inv_a, fwd_c=fwd_c, inv_c=inv_c,
                tw_by_n2=(twr.T[:, :, None], twi.T[:, :, None]),
                tw_by_k1=(twr[:, :, None], twi[:, :, None]))


HALO = 16


def _hy_short_kernel(x1c, x1p, x1n, x2c, x2p, x2n, vc, vp, vn, w_ref, b_ref, vv_ref, x1o_ref):
    i = pl.program_id(1)
    last = pl.num_programs(1) - 1
    tm = x1c.shape[1]
    row = lax.broadcasted_iota(jnp.int32, (tm, 1), 0)

    def conv(cur_ref, prev_ref, next_ref, g):
        cur = cur_ref[0].astype(F32)
        before = jnp.where(i > 0, prev_ref[0, HALO - 1:HALO, :].astype(F32), 0.0)
        after = jnp.where(i < last, next_ref[0, 0:1, :].astype(F32), 0.0)
        up = jnp.where(row == 0, before, pltpu.roll(cur, 1, axis=0))
        dn = jnp.where(row == tm - 1, after, pltpu.roll(cur, tm - 1, axis=0))
        return (w_ref[0, g:g + 1, :] * up + w_ref[1, g:g + 1, :] * cur + w_ref[2, g:g + 1, :] * dn
                + b_ref[g:g + 1, :])

    x1 = conv(x1c, x1p, x1n, 0)
    x2 = conv(x2c, x2p, x2n, 1)
    v = conv(vc, vp, vn, 2)
    vv_ref[0] = (v * x2).astype(vv_ref.dtype)
    x1o_ref[0] = x1.astype(x1o_ref.dtype)


def _hy_short(proj, conv_w, conv_b, d_hy):
    bsz, seq, _ = proj.shape
    tm = _largest_tile(seq, 1024, HALO)
    cb = _largest_tile(d_hy, 512, LANES)
    ncb = d_hy // cb
    nh = seq // HALO

    def specs(g):
        return [
            pl.BlockSpec((1, tm, cb), lambda b, i, c: (b, i, g * ncb + c)),
            pl.BlockSpec((1, HALO, cb), lambda b, i, c: (b, jnp.maximum(i * (tm // HALO) - 1, 0), g * ncb + c)),
            pl.BlockSpec((1, HALO, cb), lambda b, i, c: (b, jnp.minimum((i + 1) * (tm // HALO), nh - 1),
                                                         g * ncb + c)),
        ]

    out_spec = pl.BlockSpec((1, tm, cb), lambda b, i, c: (b, i, c))
    return pl.pallas_call(
        _hy_short_kernel,
        grid=(bsz, seq // tm, ncb),
        in_specs=specs(0) + specs(1) + specs(2) + [
            pl.BlockSpec((3, 3, cb), lambda b, i, c: (0, 0, c)),
            pl.BlockSpec((3, cb), lambda b, i, c: (0, c)),
        ],
        out_specs=[out_spec, out_spec],
        out_shape=[jax.ShapeDtypeStruct((bsz, seq, d_hy), BF16)] * 2,
        compiler_params=_cparams(("parallel", "parallel", "parallel")),
        name="hy_short",
    )(*([proj] * 9), conv_w.reshape(3, 3, d_hy), conv_b.reshape(3, d_hy))


def _hy_fwd_a_kernel(v_ref, fa_ref, twr_ref, twi_ref, o_ref, *, c):
    n1 = fa_ref.shape[0] // 2
    res = jnp.dot(fa_ref[...], v_ref[0], preferred_element_type=F32)
    for s in range(v_ref.shape[2] // c):
        cols = slice(s * c, (s + 1) * c)
        ar, ai = res[:n1, cols], res[n1:, cols]
        tr, ti = twr_ref[s], twi_ref[s]
        o_ref[0, 0, :, cols] = (ar * tr - ai * ti).astype(o_ref.dtype)
        o_ref[0, 1, :, cols] = (ar * ti + ai * tr).astype(o_ref.dtype)


def _hy_fwd_a(vv, tabs):
    bsz, seq, c = vv.shape
    n2 = DFT_N2
    n1 = 2 * seq // n2
    per = max(1, DFT_COLS // c)
    cols = per * c
    twr, twi = tabs["tw_by_n2"]
    return pl.pallas_call(
        functools.partial(_hy_fwd_a_kernel, c=c),
        grid=(bsz, n2 // per),
        in_specs=[
            pl.BlockSpec((1, n1 // 2, cols), lambda b, j: (b, 0, j)),
            pl.BlockSpec((2 * n1, n1 // 2), lambda b, j: (0, 0)),
            pl.BlockSpec((per, n1, 1), lambda b, j: (j, 0, 0)),
            pl.BlockSpec((per, n1, 1), lambda b, j: (j, 0, 0)),
        ],
        out_specs=pl.BlockSpec((1, 2, n1, cols), lambda b, j: (b, 0, 0, j)),
        out_shape=jax.ShapeDtypeStruct((bsz, 2, n1, n2 * c), BF16),
        compiler_params=_cparams(("parallel", "parallel")),
        name="hy_fwd_a",
    )(vv.reshape(bsz, n1 // 2, n2 * c), tabs["fwd_a"], twr, twi)


K1_PER_STEP = 16


def _hy_mid_kernel(a_ref, h_ref, fc_ref, ic_ref, twr_ref, twi_ref, o_ref):
    n2 = DFT_N2
    for k in range(a_ref.shape[2]):
        a = jnp.concatenate([a_ref[0, 0, k], a_ref[0, 1, k]], axis=0)
        x = jnp.dot(fc_ref[...], a, preferred_element_type=F32)
        xr, xi = x[:n2], x[n2:]
        hr, hi = h_ref[0, k].astype(F32), h_ref[1, k].astype(F32)
        y = jnp.concatenate([xr * hr - xi * hi, xr * hi + xi * hr], axis=0).astype(BF16)
        z = jnp.dot(ic_ref[...], y, preferred_element_type=F32)
        zr, zi = z[:n2], z[n2:]
        tr, ti = twr_ref[k], twi_ref[k]
        o_ref[0, 0, k] = (zr * tr + zi * ti).astype(o_ref.dtype)
        o_ref[0, 1, k] = (zi * tr - zr * ti).astype(o_ref.dtype)


def _hy_mid(a, spec_l, tabs):
    bsz, _, n1, n2c = a.shape
    n2 = DFT_N2
    c = n2c // n2
    cb = _largest_tile(c, MID_CHANNELS, LANES)
    kc = _largest_tile(n1, K1_PER_STEP, 1)
    twr, twi = tabs["tw_by_k1"]
    split = lambda t: jnp.transpose(t.reshape(bsz, 2, n1, n2, c // cb, cb), (0, 1, 2, 4, 3, 5))
    z = pl.pallas_call(
        _hy_mid_kernel,
        grid=(c // cb, n1 // kc, bsz),
        in_specs=[
            pl.BlockSpec((1, 2, kc, None, n2, cb), lambda j, k, b: (b, 0, k, j, 0, 0)),
            pl.BlockSpec((2, kc, n2, cb), lambda j, k, b: (0, k, 0, j)),
            pl.BlockSpec((2 * n2, 2 * n2), lambda j, k, b: (0, 0)),
            pl.BlockSpec((2 * n2, 2 * n2), lambda j, k, b: (0, 0)),
            pl.BlockSpec((kc, n2, 1), lambda j, k, b: (k, 0, 0)),
            pl.BlockSpec((kc, n2, 1), lambda j, k, b: (k, 0, 0)),
        ],
        out_specs=pl.BlockSpec((1, 2, kc, None, n2, cb), lambda j, k, b: (b, 0, k, j, 0, 0)),
        out_shape=jax.ShapeDtypeStruct((bsz, 2, n1, c // cb, n2, cb), BF16),
        compiler_params=_cparams(("parallel", "parallel", "parallel")),
        name="hy_mid",
    )(split(a), spec_l, tabs["fwd_c"], tabs["inv_c"], twr, twi)
    return jnp.transpose(z, (0, 1, 2, 4, 3, 5)).reshape(bsz, 2, n1, n2, c)


def _hy_spec_kernel(a_ref, fc_ref, o_ref):
    n2 = DFT_N2
    cb = a_ref.shape[-1]
    for k in range(a_ref.shape[2]):
        fwd = jnp.concatenate([a_ref[0, 0, k], a_ref[0, 1, k]], axis=0)
        bwd = jnp.concatenate([a_ref[1, 0, k], a_ref[1, 1, k]], axis=0)
        x = jnp.dot(fc_ref[...], jnp.concatenate([fwd, bwd], axis=1), preferred_element_type=F32)
        o_ref[0, k] = (x[:n2, :cb] + x[:n2, cb:]).astype(o_ref.dtype)
        o_ref[1, k] = (x[n2:, :cb] - x[n2:, cb:]).astype(o_ref.dtype)


def _hy_spectrum(taps, tabs):
    a = _hy_fwd_a(taps, tabs)
    _, _, n1, n2c = a.shape
    n2 = DFT_N2
    c = n2c // n2
    cb = _largest_tile(c, MID_CHANNELS, LANES)
    kc = _largest_tile(n1, K1_PER_STEP, 1)
    return pl.pallas_call(
        _hy_spec_kernel,
        grid=(c // cb, n1 // kc),
        in_specs=[
            pl.BlockSpec((2, 2, kc, n2, cb), lambda j, k: (0, 0, k, 0, j)),
            pl.BlockSpec((2 * n2, 2 * n2), lambda j, k: (0, 0)),
        ],
        out_specs=pl.BlockSpec((2, kc, n2, cb), lambda j, k: (0, k, 0, j)),
        out_shape=jax.ShapeDtypeStruct((2, n1, n2, c), BF16),
        compiler_params=_cparams(("parallel", "parallel")),
        name="hy_spectrum",
    )(a.reshape(2, 2, n1, n2, c), tabs["fwd_c"])


def _hy_inv_a_kernel(z_ref, ga_ref, vv_ref, x1_ref, skip_ref, o_ref):
    y = jnp.dot(ga_ref[...], z_ref[0], preferred_element_type=F32)
    v = vv_ref[0].astype(F32)
    o_ref[0] = ((y + skip_ref[...] * v) * x1_ref[0].astype(F32)).astype(o_ref.dtype)


def _hy_inv_a(z, vv, x1c, skip, tabs):
    bsz, seq, c = vv.shape
    n2 = DFT_N2
    n1 = 2 * seq // n2
    per = max(1, DFT_COLS // c)
    cols = per * c
    flat = lambda t: t.reshape(bsz, n1 // 2, n2 * c)
    data = pl.BlockSpec((1, n1 // 2, cols), lambda b, j: (b, 0, j))
    out = pl.pallas_call(
        _hy_inv_a_kernel,
        grid=(bsz, n2 // per),
        in_specs=[
            pl.BlockSpec((1, 2 * n1, cols), lambda b, j: (b, 0, j)),
            pl.BlockSpec((n1 // 2, 2 * n1), lambda b, j: (0, 0)),
            data, data,
            pl.BlockSpec((1, cols), lambda b, j: (0, 0)),
        ],
        out_specs=data,
        out_shape=jax.ShapeDtypeStruct((bsz, n1 // 2, n2 * c), BF16),
        compiler_params=_cparams(("parallel", "parallel")),
        name="hy_inv_a",
    )(z.reshape(bsz, 2 * n1, n2 * c), tabs["inv_a"], flat(vv), flat(x1c), jnp.tile(skip, per).reshape(1, cols))
    return out.reshape(bsz, seq, c)


def _hyena_mixer(proj, conv_w, conv_b, spec_l, skip, tabs, d_hy):
    vv, x1c = _hy_short(proj, conv_w, conv_b, d_hy)
    a = _hy_fwd_a(vv, tabs)
    z = _hy_mid(a, spec_l, tabs)
    return _hy_inv_a(z, vv, x1c, skip, tabs)


def _route_kernel(a_ref, upper_ref, lower_ref, sel_ref, idx_ref, slot_scr, *, cap):
    a = a_ref[0]
    n_rows = a.shape[0]
    bits = pltpu.bitcast(a, jnp.int32)
    thr = jnp.int32(0)
    for b in range(30, -1, -1):
        cand = thr | jnp.int32(1 << b)
        cnt = jnp.sum(jnp.where(bits >= cand, 1.0, 0.0))
        thr = jnp.where(cnt >= cap, cand, thr)
    gt = jnp.where(bits > thr, 1.0, 0.0)
    eq = jnp.where(bits == thr, 1.0, 0.0)
    need = cap - jnp.sum(gt)

    def prefix(m):
        within = jnp.dot(m.astype(BF16), upper_ref[...], preferred_element_type=F32)
        totals = jnp.broadcast_to(within[:, LANES - 1:LANES], within.shape).astype(BF16)
        return within + jnp.dot(lower_ref[...], totals, preferred_element_type=F32)

    sel = gt + eq * jnp.where(prefix(eq) <= need, 1.0, 0.0)
    slot = jnp.where(sel > 0.0, prefix(sel) - 1.0, -1.0)
    sel_ref[0] = slot.astype(jnp.int32)
    slot_scr[...] = slot

    out_rows = idx_ref.shape[1]
    row_id = lax.broadcasted_iota(jnp.int32, (out_rows, 2 * LANES), 0).astype(F32)
    col_id = lax.broadcasted_iota(jnp.int32, (LANES, 2 * LANES), 0).astype(F32)
    lane = lax.broadcasted_iota(jnp.int32, (1, LANES), 1).astype(F32)
    lanes2 = jnp.concatenate([lane, lane], axis=1)

    def body(g, acc):
        pair = slot_scr[pl.ds(2 * g, 2), :]
        s = jnp.concatenate([pair[0:1], pair[1:2]], axis=1)
        hi = jnp.floor(s * (1.0 / LANES))
        lo = s - hi * LANES
        base = jnp.asarray(2 * g).astype(F32)
        rows = jnp.concatenate([jnp.full((1, LANES), 0.0, F32), jnp.full((1, LANES), 1.0, F32)], axis=1) + base
        in_row = hi == row_id
        lhs = jnp.concatenate([jnp.where(in_row, rows, 0.0), jnp.where(in_row, lanes2, 0.0)], axis=0)
        rhs = jnp.where(lo == col_id, 1.0, 0.0)
        return acc + lax.dot_general(lhs.astype(BF16), rhs.astype(BF16), (((1,), (1,)), ((), ())),
                                     preferred_element_type=F32)

    acc = lax.fori_loop(0, n_rows // 2, body, jnp.zeros((2 * out_rows, LANES), F32), unroll=4)
    idx_ref[0] = (acc[:out_rows] * LANES + acc[out_rows:]).astype(jnp.int32)


def _route(aff_t, cap):
    n_exp, n_tok = aff_t.shape
    n_rows = n_tok // LANES
    assert n_tok % (2 * LANES) == 0 and cap % LANES == 0
    assert n_rows <= BF16_EXACT_INT
    i = jnp.arange(LANES)
    upper = (i[:, None] <= i[None, :]).astype(BF16)
    r = jnp.arange(n_rows)
    lower = (r[:, None] > r[None, :]).astype(BF16)
    sel, idx = pl.pallas_call(
        functools.partial(_route_kernel, cap=cap),
        grid=(n_exp,),
        in_specs=[
            pl.BlockSpec((1, n_rows, LANES), lambda e: (e, 0, 0)),
            pl.BlockSpec((LANES, LANES), lambda e: (0, 0)),
            pl.BlockSpec((n_rows, n_rows), lambda e: (0, 0)),
        ],
        out_specs=[
            pl.BlockSpec((1, n_rows, LANES), lambda e: (e, 0, 0)),
            pl.BlockSpec((1, cap // LANES, LANES), lambda e: (e, 0, 0)),
        ],
        out_shape=[
            jax.ShapeDtypeStruct((n_exp, n_rows, LANES), jnp.int32),
            jax.ShapeDtypeStruct((n_exp, cap // LANES, LANES), jnp.int32),
        ],
        scratch_shapes=[pltpu.VMEM((n_rows, LANES), F32)],
        compiler_params=_cparams(("parallel",)),
        name="route",
    )(aff_t.reshape(n_exp, n_rows, LANES), upper, lower)
    return sel.reshape(n_exp, n_tok), idx.reshape(n_exp, cap)


COMBINE_TOKENS = 512
WINDOW = 128
ROW_ALIGN = 16


def _combine_kernel(lo_ref, nwin_ref, slot_ref, x1_ref, g2_ref, ye_hbm, o_ref, buf, acc_ref, sem,
                    *, n_exp, cap):
    i = pl.program_id(0)
    n_tiles = pl.num_programs(0)

    def window_start(tile, e, k):
        return pl.multiple_of(jnp.minimum(lo_ref[tile * n_exp + e] + k * WINDOW, cap - WINDOW), ROW_ALIGN)

    def copy(tile, e, k, slot):
        return pltpu.make_async_copy(ye_hbm.at[e, pl.ds(window_start(tile, e, k), WINDOW), :],
                                     buf.at[slot, pl.ds(e * WINDOW, WINDOW), :], sem.at[slot])

    def fetch(tile, k, slot):
        for e in range(n_exp):
            copy(tile, e, k, slot).start()

    def wait(tile, k, slot):
        for e in range(n_exp):
            copy(tile, e, k, slot).wait()

    def placed(k, slot):
        slots = slot_ref[...]
        j = lax.broadcasted_iota(jnp.int32, (slots.shape[0], WINDOW), 1)
        cols = []
        for e in range(n_exp):
            s = slots[:, e:e + 1]
            lo = lo_ref[i * n_exp + e] + k * WINDOW
            row = jnp.where(s >= lo, s - window_start(i, e, k), -1)
            cols.append(jnp.where(row == j, 1.0, 0.0).astype(BF16))
        return jnp.dot(jnp.concatenate(cols, axis=1), buf[slot], preferred_element_type=F32)

    cur = lax.rem(i, 2)

    @pl.when(i == 0)
    def _():
        fetch(0, 0, 0)

    @pl.when(i + 1 < n_tiles)
    def _():
        fetch(i + 1, 0, 1 - cur)

    wait(i, 0, cur)
    acc_ref[...] = placed(0, cur)

    def extra(k, carry):
        fetch(i, k, 2)
        wait(i, k, 2)
        acc_ref[...] += placed(k, 2)
        return carry

    lax.fori_loop(1, nwin_ref[i], extra, 0)
    o_ref[...] = x1_ref[...] + g2_ref[0] * acc_ref[...]


def _combine(x1, g2, ye, slot):
    bsz, seq, d = x1.shape
    n_exp, cap, _ = ye.shape
    n_tok = bsz * seq
    t = _largest_tile(seq, COMBINE_TOKENS, ROW_ALIGN)
    n_tiles = n_tok // t
    assert cap >= WINDOW and cap % ROW_ALIGN == 0
    cnt = (slot >= 0).astype(jnp.int32).reshape(n_exp, n_tiles, t).sum(axis=2)
    first = jnp.cumsum(cnt, axis=1) - cnt
    lo = (first // ROW_ALIGN) * ROW_ALIGN
    nwin = jnp.maximum(jnp.max(-(-(first - lo + cnt) // WINDOW), axis=0), 1)
    tiles_per_seq = seq // t
    grid_spec = pltpu.PrefetchScalarGridSpec(
        num_scalar_prefetch=2,
        grid=(n_tiles,),
        in_specs=[
            pl.BlockSpec((t, n_exp), lambda i, lo_r, nw_r: (i, 0)),
            pl.BlockSpec((t, d), lambda i, lo_r, nw_r: (i, 0)),
            pl.BlockSpec((1, 1, d), lambda i, lo_r, nw_r: (i // tiles_per_seq, 0, 0)),
            pl.BlockSpec(memory_space=pl.ANY),
        ],
        out_specs=pl.BlockSpec((t, d), lambda i, lo_r, nw_r: (i, 0)),
        scratch_shapes=[
            pltpu.VMEM((3, n_exp * WINDOW, d), BF16),
            pltpu.VMEM((t, d), F32),
            pltpu.SemaphoreType.DMA((3,)),
        ],
    )
    out = pl.pallas_call(
        functools.partial(_combine_kernel, n_exp=n_exp, cap=cap),
        grid_spec=grid_spec,
        out_shape=jax.ShapeDtypeStruct((n_tok, d), F32),
        compiler_params=_cparams(("arbitrary",)),
        name="combine",
    )(lo.T.reshape(-1).astype(jnp.int32), nwin.astype(jnp.int32), slot.T.astype(jnp.int32),
      x1.reshape(n_tok, d), g2[:, None, :], ye)
    return out.reshape(bsz, seq, d)


def _layer(x, ada, p):
    bsz, seq, d = x.shape
    d_hy = p["hy_skip"].shape[0]
    d_att = p["out_norm_att"].shape[0]
    n_exp = p["w_router"].shape[1]
    sh1, sc1, g1, sh2, sc2, g2 = jnp.split(ada, N_ADA, axis=-1)

    q_gain = p["q_norm_g"] * (HEAD_DIM ** -0.5)
    proj = _inproj(x, p["norm1_g"], sc1, sh1, p["w_in_bf16"], q_gain, p["k_norm_g"], d_hy, d_att)

    y_hy = _hyena_mixer(proj, p["hy_conv_w"], p["hy_conv_b"], p["spec"], p["hy_skip"], p["dft"], d_hy)
    y_at = _natten(proj, p["bias_tab"], d_hy, d_att)

    x1, h2, aff = _outproj(y_hy, y_at, x, p["out_norm_hy"], p["out_norm_att"], p["w_out_bf16"],
                           g1, p["norm2_g"], sc2, sh2, p["w_router"])

    n_tok = bsz * seq
    cap = CAPACITY_FACTOR * n_tok // n_exp
    aff_t = aff.reshape(n_tok, n_exp).T
    slot, idx = _route(aff_t, cap)
    gates = jnp.take_along_axis(aff_t, idx, axis=1)
    xg = jnp.take(h2.reshape(n_tok, d), idx, axis=0)
    ye = _expert_ffn(xg, p["wg"], p["wu"], p["wd"], gates)
    return _combine(x1, g2, ye, slot)


def kernel(x_prompt, x_sample, c_prompt, c_sample, ada_w, ada_b, norm1_g, w_in, hy_conv_w, hy_conv_b, hy_f_w0, hy_f_b0, hy_f_w1, hy_f_b1, hy_f_w2, hy_f_b2, hy_f_w3, hy_f_freq, hy_skip, q_norm_g, k_norm_g, rpb, out_norm_hy, out_norm_att, w_out, norm2_g, w_router, w_gate, w_up, w_down):
    depth = ada_w.shape[0]
    y_prompt, y_sample = x_prompt, x_sample
    nbp = x_prompt.shape[0]
    assert x_prompt.shape[1] == x_sample.shape[1]
    seq = x_prompt.shape[1]
    for l in range(depth):
        d_hy = hy_skip.shape[-1]
        dft = _dft_tables(seq)
        taps = _hyena_filter_taps(seq, hy_f_w0[l], hy_f_b0[l], hy_f_w1[l], hy_f_b1[l], hy_f_w2[l],
                                  hy_f_b2[l], hy_f_w3[l], hy_f_freq[l], d_hy)
        p = {
            "norm1_g": norm1_g[l], "w_in_bf16": w_in[l].astype(BF16),
            "hy_conv_w": hy_conv_w[l], "hy_conv_b": hy_conv_b[l], "hy_skip": hy_skip[l],
            "spec": _hy_spectrum(taps, dft), "dft": dft,
            "q_norm_g": q_norm_g[l], "k_norm_g": k_norm_g[l], "bias_tab": _natten_bias(rpb[l], _natten_heads(rpb.shape[1])),
            "out_norm_hy": out_norm_hy[l], "out_norm_att": out_norm_att[l],
            "w_out_bf16": w_out[l].astype(BF16), "norm2_g": norm2_g[l], "w_router": w_router[l],
            "wg": _to_bf16(w_gate[l]), "wu": _to_bf16(w_up[l]), "wd": _to_bf16(w_down[l]),
        }
        ada = _ada(jnp.concatenate([c_prompt, c_sample], axis=0), ada_w[l], ada_b[l])
        y_prompt = _layer(y_prompt, ada[:nbp], p)
        y_sample = _layer(y_sample, ada[nbp:], p)
    return (y_prompt, y_sample)
```
